```python
import math
import jax, jax.numpy as jnp
from jax import lax
import numpy as np

D_MODEL = 4096
BATCH = 32
SEQ = 256
DEPTH = 1
DEC_BATCH = 8
DEC_SEQ = 4096
PAST_LEN = 512

GRID_W = 64
F_WIDTH = D_MODEL // 2
F_GROUPS = 4
F_GC = F_WIDTH // F_GROUPS
GLA_WIDTH = D_MODEL - F_WIDTH
GLA_HEADS = 4
GLA_DK_TOTAL = GLA_WIDTH // 2
GLA_DK = GLA_DK_TOTAL // GLA_HEADS
GLA_DV = GLA_WIDTH // GLA_HEADS
GLA_LOWRANK = 16
GLA_TAU = 16.0
GLA_CHUNK = 64
EPS = 1e-6
IN_SIZES = (F_WIDTH, F_WIDTH, GLA_DK_TOTAL, GLA_DK_TOTAL, GLA_WIDTH, GLA_WIDTH, GLA_LOWRANK, GLA_LOWRANK)
N_IN = sum(IN_SIZES)

kernel_name = "hymba_fnet_gla_diffusion_step"


def _split_offsets():
    offs, acc = [], 0
    for s in IN_SIZES[:-1]:
        acc += s
        offs.append(acc)
    return offs


def rmsnorm(x, w):
    xf = x.astype(jnp.float32)
    y = xf * lax.rsqrt(jnp.mean(xf * xf, axis=-1, keepdims=True) + EPS)
    return (y * w.astype(jnp.float32)).astype(x.dtype)


def fourier_mix(u, w_f, rows):
    B, L, _ = u.shape
    uf = u.astype(jnp.float32)
    if rows is None:
        spec = jnp.fft.fftn(uf.reshape(B, L, F_GROUPS, F_GC), axes=(1, 3), norm="ortho")
    else:
        spec = jnp.fft.fftn(uf.reshape(B, rows, GRID_W, F_GROUPS, F_GC), axes=(1, 2, 4), norm="ortho")
    re = jnp.real(spec).reshape(B, L, F_GROUPS, F_GC).astype(u.dtype)
    out = jnp.einsum('blgc,gcd->blgd', re, w_f)
    return out.reshape(B, L, F_WIDTH)


def gla_scan(q, k, v, g, s0):
    B, H, L, dk = q.shape
    dv = v.shape[-1]
    n = L // GLA_CHUNK

    def chunks(t):
        return jnp.moveaxis(t.reshape(B, H, n, GLA_CHUNK, t.shape[-1]), 2, 0)

    mask = jnp.tril(jnp.ones((GLA_CHUNK, GLA_CHUNK), dtype=bool))

    def step(S, inp):
        qc, kc, vc, gc = inp
        b = jnp.cumsum(gc, axis=-2)
        b_last = b[..., -1:, :]
        q_t = qc * jnp.exp(b)
        k_t = kc * jnp.exp(-b)
        o_inter = jnp.einsum('bhik,bhkv->bhiv', q_t, S)
        att = jnp.where(mask, jnp.einsum('bhik,bhjk->bhij', q_t, k_t), 0.0)
        o_intra = jnp.einsum('bhij,bhjv->bhiv', att, vc)
        k_dec = kc * jnp.exp(b_last - b)
        S_new = jnp.exp(b_last)[..., 0, :, None] * S + jnp.einsum('bhjk,bhjv->bhkv', k_dec, vc)
        return S_new.astype(S.dtype), (o_inter + o_intra).astype(vc.dtype)

    S_fin, o = lax.scan(step, s0, (chunks(q), chunks(k), chunks(v), chunks(g)))
    o = jnp.moveaxis(o, 0, 2).reshape(B, H, L, dv)
    return o, S_fin


def gla_bidir(q, k, v, g_f, g_b, s0_f, s0_b):
    o_f, s_f = gla_scan(q, k, v, g_f, s0_f)
    flip = lambda t: jnp.flip(t, axis=2)
    o_b, s_b = gla_scan(flip(q), flip(k), flip(v), flip(g_b), s0_b)
    return o_f + flip(o_b), s_f, s_b


def to_heads(t, d):
    B, L = t.shape[:2]
    return t.reshape(B, L, GLA_HEADS, d).transpose(0, 2, 1, 3)


def mixer_branch(h, w_in, w_alpha, b_alpha, w_fourier, gla_norm_w, w_out, rows, s0_f, s0_b):
    B, L, _ = h.shape
    proj = h @ w_in
    f_in, f_gate, q, k, v, g_gate, a_f, a_b = jnp.split(proj, _split_offsets(), axis=-1)
    f_out = fourier_mix(f_in, w_fourier, rows) * jax.nn.silu(f_gate)
    qh = to_heads(q, GLA_DK) * (GLA_DK ** -0.5)
    kh = to_heads(k, GLA_DK)
    vh = to_heads(v, GLA_DV)
    w32 = w_alpha.astype(jnp.float32)
    b32 = b_alpha.astype(jnp.float32)
    g_f = jax.nn.log_sigmoid(a_f.astype(jnp.float32) @ w32[0] + b32[0]) / GLA_TAU
    g_b = jax.nn.log_sigmoid(a_b.astype(jnp.float32) @ w32[1] + b32[1]) / GLA_TAU
    o, s_f, s_b = gla_bidir(qh, kh, vh, to_heads(g_f, GLA_DK), to_heads(g_b, GLA_DK), s0_f, s0_b)
    o = rmsnorm(o.transpose(0, 2, 1, 3), gla_norm_w)
    g_out = o.reshape(B, L, GLA_WIDTH) * jax.nn.silu(g_gate)
    out = jnp.concatenate([f_out, g_out], axis=-1) @ w_out
    return out, s_f, s_b


def setup_inputs(seed: int = 0) -> dict:
    key = jax.random.key(seed)
    ks = jax.random.split(key, 16)
    f32 = jnp.float32
    nrm = lambda k, shape, s: jax.random.normal(k, shape, f32) * s
    return {
        "x_prompt": nrm(ks[0], (BATCH, SEQ, D_MODEL), 1.0),
        "x_sample": nrm(ks[1], (DEC_BATCH, DEC_SEQ, D_MODEL), 1.0),
        "state_gla_fwd": nrm(ks[2], (DEC_BATCH, DEPTH, GLA_HEADS, GLA_DK, GLA_DV), 1.0),
        "state_gla_bwd": nrm(ks[3], (DEC_BATCH, DEPTH, GLA_HEADS, GLA_DK, GLA_DV), 1.0),
        "c": nrm(ks[4], (DEC_BATCH, D_MODEL), 1.0),
        "c_ctx": nrm(ks[5], (D_MODEL,), 1.0),
        "ada_w": nrm(ks[6], (DEPTH, D_MODEL, 3 * D_MODEL), D_MODEL ** -0.5),
        "ada_b": nrm(ks[7], (DEPTH, 3 * D_MODEL), 0.02),
        "norm_w": 1.0 + nrm(ks[8], (DEPTH, D_MODEL), 0.02),
        "w_in": nrm(ks[9], (DEPTH, D_MODEL, N_IN), D_MODEL ** -0.5),
        "w_alpha": nrm(ks[10], (DEPTH, 2, GLA_LOWRANK, GLA_DK_TOTAL), GLA_LOWRANK ** -0.5),
        "b_alpha": nrm(ks[11], (DEPTH, 2, GLA_DK_TOTAL), 0.5),
        "w_fourier": nrm(ks[12], (DEPTH, F_GROUPS, F_GC, F_GC), F_GC ** -0.5),
        "gla_norm_w": 1.0 + nrm(ks[13], (DEPTH, GLA_HEADS, GLA_DV), 0.02),
        "w_out": nrm(ks[14], (DEPTH, D_MODEL, D_MODEL), D_MODEL ** -0.5),
        "final_norm_w": 1.0 + nrm(ks[15], (D_MODEL,), 0.02),
    }


def reference(x_prompt, x_sample, state_gla_fwd, state_gla_bwd, c, c_ctx, ada_w, ada_b, norm_w,
              w_in, w_alpha, b_alpha, w_fourier, gla_norm_w, w_out, final_norm_w):
    Bp = x_prompt.shape[0]
    rows = x_sample.shape[1] // GRID_W
    zeros_state = jnp.zeros((Bp, GLA_HEADS, GLA_DK, GLA_DV), dtype=x_prompt.dtype)
    xp, xs = x_prompt, x_sample
    new_f, new_b = [], []
    for l in range(DEPTH):
        mod_ctx = jax.nn.silu(c_ctx) @ ada_w[l] + ada_b[l]
        sh, sc, gt = jnp.split(mod_ctx, 3, axis=-1)
        h = rmsnorm(xp, norm_w[l]) * (1.0 + sc) + sh
        out, s_f, s_b = mixer_branch(h, w_in[l], w_alpha[l], b_alpha[l], w_fourier[l], gla_norm_w[l],
                                     w_out[l], None, zeros_state, zeros_state)
        xp = xp + gt * out
        new_f.append(s_f)
        new_b.append(s_b)
        mod = jax.nn.silu(c) @ ada_w[l] + ada_b[l]
        sh, sc, gt = jnp.split(mod[:, None, :], 3, axis=-1)
        h = rmsnorm(xs, norm_w[l]) * (1.0 + sc) + sh
        out, _, _ = mixer_branch(h, w_in[l], w_alpha[l], b_alpha[l], w_fourier[l], gla_norm_w[l],
                                 w_out[l], rows, state_gla_fwd[:, l], state_gla_bwd[:, l])
        xs = xs + gt * out
    y_prompt = rmsnorm(xp, final_norm_w)
    y_sample = rmsnorm(xs, final_norm_w)
    new_state_fwd = jnp.stack(new_f, axis=1)
    new_state_bwd = jnp.stack(new_b, axis=1)
    return (y_prompt, y_sample, new_state_fwd, new_state_bwd)
```

```python
import functools

import numpy as np
import jax
import jax.numpy as jnp
from jax import lax
from jax.experimental import pallas as pl
from jax.experimental.pallas import tpu as pltpu

F32 = jnp.float32
BF16 = jnp.bfloat16

GRID_W = 64
F_GROUPS = 4
GLA_HEADS = 4
GLA_LOWRANK = 16
GLA_TAU = 16.0
GLA_CHUNK = 64
EPS = 1e-6

VMEM_LIMIT_BYTES = 56 * 1024 * 1024
LANES = 128
MOD_ROWS = 16


def _params(*sem):
    return pltpu.CompilerParams(dimension_semantics=sem, vmem_limit_bytes=VMEM_LIMIT_BYTES)


def _dot(a, b):
    return jnp.dot(a, b, preferred_element_type=F32)


def _split3(x):
    hi = x.astype(BF16)
    r1 = x - hi.astype(F32)
    mid = r1.astype(BF16)
    lo = (r1 - mid.astype(F32)).astype(BF16)
    return hi, mid, lo


def _split2(x):
    hi = x.astype(BF16)
    lo = (x - hi.astype(F32)).astype(BF16)
    return hi, lo


def _dft_mats(n):
    k = np.arange(n)
    ang = 2.0 * np.pi * ((k[:, None] * k[None, :]) % n) / n
    s = 1.0 / np.sqrt(n)
    return (np.cos(ang) * s).astype(np.float32), (np.sin(ang) * s).astype(np.float32)


def _mod_kernel(c_ref, w_ref, b_ref, o_ref):
    c = c_ref[...]
    s = c * jax.nn.sigmoid(c)
    s_hi, s_lo = _split2(s)
    w_hi, w_lo = _split2(w_ref[...])
    o_ref[...] = _dot(s_hi, w_hi) + _dot(s_hi, w_lo) + _dot(s_lo, w_hi) + b_ref[...]


def _modulation(cc, ada_w, ada_b, tn=512):
    d, n = ada_w.shape
    return pl.pallas_call(
        _mod_kernel,
        grid=(n // tn,),
        in_specs=[
            pl.BlockSpec((MOD_ROWS, d), lambda j: (0, 0)),
            pl.BlockSpec((d, tn), lambda j: (0, j)),
            pl.BlockSpec((1, tn), lambda j: (0, j)),
        ],
        out_specs=pl.BlockSpec((MOD_ROWS, tn), lambda j: (0, j)),
        out_shape=jax.ShapeDtypeStruct((MOD_ROWS, n), F32),
        compiler_params=_params("parallel"),
        name="mod",
    )(cc, ada_w, ada_b)


def _inproj_kernel(x_ref, mod_ref, nw_ref, w_ref, wa_ref, p_ref, a_ref, h_ref, *, row_chunk):
    tm = x_ref.shape[0]

    @pl.when(pl.program_id(1) == 0)
    def _():
        sh = mod_ref[0:1, :]
        sc1 = 1.0 + mod_ref[1:2, :]
        nw = nw_ref[...]

        def body(r, carry):
            rs = pl.ds(pl.multiple_of(r * row_chunk, row_chunk), row_chunk)
            x = x_ref[rs, :]
            ms = jnp.mean(x * x, axis=-1, keepdims=True)
            y = (x * lax.rsqrt(ms + EPS)) * nw
            h_ref[rs, :] = (y * sc1 + sh).astype(BF16)
            return carry

        lax.fori_loop(0, tm // row_chunk, body, 0)
        a_ref[...] = _dot(h_ref[...], wa_ref[...])

    p_ref[...] = _dot(h_ref[...], w_ref[...]).astype(BF16)


def _inproj(x2, mod3, nw, w_bf, wa_bf, mod_row, name, tm=1024, tn=512):
    t, d = x2.shape
    n = w_bf.shape[1]
    return pl.pallas_call(
        functools.partial(_inproj_kernel, row_chunk=128),
        grid=(t // tm, n // tn),
        in_specs=[
            pl.BlockSpec((tm, d), lambda i, j: (i, 0)),
            pl.BlockSpec((None, 3, d), lambda i, j: (mod_row(i), 0, 0)),
            pl.BlockSpec((1, d), lambda i, j: (0, 0)),
            pl.BlockSpec((d, tn), lambda i, j: (0, j)),
            pl.BlockSpec((d, LANES), lambda i, j: (0, 0)),
        ],
        out_specs=[
            pl.BlockSpec((tm, tn), lambda i, j: (i, j)),
            pl.BlockSpec((tm, LANES), lambda i, j: (i, 0)),
        ],
        out_shape=[
            jax.ShapeDtypeStruct((t, n), BF16),
            jax.ShapeDtypeStruct((t, LANES), F32),
        ],
        scratch_shapes=[pltpu.VMEM((tm, d), BF16)],
        compiler_params=_params("parallel", "arbitrary"),
        name=name,
    )(x2, mod3, nw, w_bf, wa_bf)


def _gla_kernel(*refs, seq_len, has_state, emit_state):
    q_ref, k_ref, v_ref, gg_ref, a_ref, wal_hi_ref, wal_lo_ref, bal_ref, gnw_ref = refs[:9]
    pos = 9
    if has_state:
        s0f_ref, s0b_ref = refs[pos:pos + 2]
        pos += 2
    o_ref = refs[pos]
    pos += 1
    if emit_state:
        sf_ref, sb_ref = refs[pos:pos + 2]
        pos += 2
    of_ref, s_ref = refs[pos:pos + 2]

    c = GLA_CHUNK
    n_chunks = seq_len // c
    dk = q_ref.shape[-1]
    q_scale = float(dk) ** -0.5
    row = lax.broadcasted_iota(jnp.int32, (c, c), 0)
    col = lax.broadcasted_iota(jnp.int32, (c, c), 1)
    gnw = gnw_ref[...]

    def chunk(ci, direction):
        rs = pl.ds(pl.multiple_of(ci * c, c), c)
        mask = (row >= col) if direction == 0 else (col >= row)
        tri = mask.astype(BF16)
        a_hi, a_lo = _split2(a_ref[rs, :])
        w_hi = wal_hi_ref[direction]
        z = _dot(a_hi, w_hi) + _dot(a_hi, wal_lo_ref[direction]) + _dot(a_lo, w_hi) + bal_ref[direction]
        g = jax.nn.log_sigmoid(z) * (1.0 / GLA_TAU)
        g_hi, g_mid, g_lo = _split3(g)
        b = _dot(tri, g_hi) + _dot(tri, g_mid) + _dot(tri, g_lo)
        b_end = b[c - 1:c, :] if direction == 0 else b[0:1, :]
        q = q_ref[rs, :].astype(F32)
        k = k_ref[rs, :].astype(F32)
        v = v_ref[rs, :]
        q_t = (q * q_scale * jnp.exp(b)).astype(BF16)
        k_t = (k * jnp.exp(-b)).astype(BF16)
        k_dec = (k * jnp.exp(b_end - b)).astype(BF16)
        o = _dot(q_t, s_ref[...].astype(BF16))
        att = lax.dot_general(q_t, k_t, (((1,), (1,)), ((), ())), preferred_element_type=F32)
        att = jnp.where(mask, att, 0.0)
        o = o + _dot(att.astype(BF16), v)
        decay = jnp.exp(jnp.sum(g.T, axis=-1, keepdims=True))
        s_ref[...] = decay * s_ref[...] + lax.dot_general(
            k_dec, v, (((0,), (0,)), ((), ())), preferred_element_type=F32)
        return rs, o

    if has_state:
        s_ref[...] = s0f_ref[...]
    else:
        s_ref[...] = jnp.zeros_like(s_ref)

    def fwd_body(ci, carry):
        rs, o = chunk(ci, 0)
        of_ref[rs, :] = o
        return carry

    lax.fori_loop(0, n_chunks, fwd_body, 0)
    if emit_state:
        sf_ref[...] = s_ref[...]

    if has_state:
        s_ref[...] = s0b_ref[...]
    else:
        s_ref[...] = jnp.zeros_like(s_ref)

    def bwd_body(i, carry):
        rs, o = chunk(n_chunks - 1 - i, 1)
        o = o + of_ref[rs, :]
        ms = jnp.mean(o * o, axis=-1, keepdims=True)
        y = (o * lax.rsqrt(ms + EPS)) * gnw
        gg = gg_ref[rs, :].astype(F32)
        o_ref[rs, :] = (y * (gg * jax.nn.sigmoid(gg))).astype(BF16)
        return carry

    lax.fori_loop(0, n_chunks, bwd_body, 0)
    if emit_state:
        sb_ref[...] = s_ref[...]


def _gla(p3, a3, wal_hi, wal_lo, bal, gnw, s0f, s0b, emit_state, offsets, name):
    bsz, seq_len, _ = p3.shape
    dk = wal_hi.shape[-1] // GLA_HEADS
    dv = gnw.shape[-1]
    off_q, off_k, off_v, off_gg = offsets
    assert off_q % dk == 0 and off_k % dk == 0 and off_v % dv == 0 and off_gg % dv == 0
    qb, kb, vb, ggb = off_q // dk, off_k // dk, off_v // dv, off_gg // dv
    has_state = s0f is not None
    in_specs = [
        pl.BlockSpec((None, seq_len, dk), lambda b, h: (b, 0, qb + h)),
        pl.BlockSpec((None, seq_len, dk), lambda b, h: (b, 0, kb + h)),
        pl.BlockSpec((None, seq_len, dv), lambda b, h: (b, 0, vb + h)),
        pl.BlockSpec((None, seq_len, dv), lambda b, h: (b, 0, ggb + h)),
        pl.BlockSpec((None, seq_len, LANES), lambda b, h: (b, 0, 0)),
        pl.BlockSpec((2, LANES, dk), lambda b, h: (0, 0, h)),
        pl.BlockSpec((2, LANES, dk), lambda b, h: (0, 0, h)),
        pl.BlockSpec((2, 1, dk), lambda b, h: (0, 0, h)),
        pl.BlockSpec((None, 1, dv), lambda b, h: (h, 0, 0)),
    ]
    args = [p3, p3, p3, p3, a3, wal_hi, wal_lo, bal, gnw]
    state_spec = pl.BlockSpec((None, None, None, dk, dv), lambda b, h: (b, 0, h, 0, 0))
    if has_state:
        in_specs += [state_spec, state_spec]
        args += [s0f, s0b]
    out_specs = [pl.BlockSpec((None, seq_len, dv), lambda b, h: (b, 0, h))]
    out_shape = [jax.ShapeDtypeStruct((bsz, seq_len, GLA_HEADS * dv), BF16)]
    if emit_state:
        out_specs += [state_spec, state_spec]
        out_shape += [jax.ShapeDtypeStruct((bsz, 1, GLA_HEADS, dk, dv), F32)] * 2
    return pl.pallas_call(
        functools.partial(_gla_kernel, seq_len=seq_len, has_state=has_state, emit_state=emit_state),
        grid=(bsz, GLA_HEADS),
        in_specs=in_specs,
        out_specs=out_specs,
        out_shape=out_shape,
        scratch_shapes=[pltpu.VMEM((seq_len, dv), F32), pltpu.VMEM((dk, dv), F32)],
        compiler_params=_params("parallel", "parallel"),
        name=name,
    )(*args)


def _fourier_ctx_kernel(u_ref, cs_ref, cl_ref, sl_ref, wf_ref, o_ref):
    gc = u_ref.shape[-1]
    res = _dot(u_ref[...], cs_ref[...])
    uc = res[:, :gc].astype(BF16)
    us = res[:, gc:].astype(BF16)
    re = _dot(cl_ref[...], uc) - _dot(sl_ref[...], us)
    o_ref[...] = _dot(re.astype(BF16), wf_ref[...]).astype(BF16)


def _fourier_ctx(p3, cs, cl, sl, wf):
    bsz, seq_len, _ = p3.shape
    gc = wf.shape[-1]
    return pl.pallas_call(
        _fourier_ctx_kernel,
        grid=(bsz, F_GROUPS),
        in_specs=[
            pl.BlockSpec((None, seq_len, gc), lambda b, g: (b, 0, g)),
            pl.BlockSpec((gc, 2 * gc), lambda b, g: (0, 0)),
            pl.BlockSpec((seq_len, seq_len), lambda b, g: (0, 0)),
            pl.BlockSpec((seq_len, seq_len), lambda b, g: (0, 0)),
            pl.BlockSpec((None, gc, gc), lambda b, g: (g, 0, 0)),
        ],
        out_specs=pl.BlockSpec((None, seq_len, gc), lambda b, g: (b, 0, g)),
        out_shape=jax.ShapeDtypeStruct((bsz, seq_len, F_GROUPS * gc), BF16),
        compiler_params=_params("parallel", "parallel"),
        name="fourier_ctx",
    )(p3, cs, cl, sl, wf)


def _fourier_lat1_kernel(u_ref, csn_ref, m2_ref, zp_ref, res_ref):
    gc = u_ref.shape[-1]
    tokens = u_ref.shape[0]
    sub = 2 * GRID_W
    res_ref[...] = _dot(u_ref[...], csn_ref[...]).astype(BF16)
    for s in range(tokens // sub):
        rs = slice(s * sub, (s + 1) * sub)
        st = jnp.concatenate([res_ref[rs, :gc], res_ref[rs, gc:]], axis=0)
        out = _dot(m2_ref[...], st)
        zp_ref[0, rs, :] = out[:sub].astype(BF16)
        zp_ref[1, rs, :] = out[sub:].astype(BF16)


def _fourier_lat1(p3, csn, m2, gc, tokens=1024):
    bsz, seq_len, _ = p3.shape
    return pl.pallas_call(
        _fourier_lat1_kernel,
        grid=(bsz, seq_len // tokens, F_GROUPS),
        in_specs=[
            pl.BlockSpec((None, tokens, gc), lambda b, r, g: (b, r, g)),
            pl.BlockSpec((gc, 2 * gc), lambda b, r, g: (0, 0)),
            pl.BlockSpec((4 * GRID_W, 4 * GRID_W), lambda b, r, g: (0, 0)),
        ],
        out_specs=pl.BlockSpec((None, None, 2, tokens, gc), lambda b, r, g: (b, g, 0, r, 0)),
        out_shape=jax.ShapeDtypeStruct((bsz, F_GROUPS, 2, seq_len, gc), BF16),
        scratch_shapes=[pltpu.VMEM((tokens, 2 * gc), BF16)],
        compiler_params=_params("parallel", "parallel", "parallel"),
        name="fourier_lat1",
    )(p3, csn, m2)


def _fourier_lat2_kernel(*refs, n_cols):
    z_refs = refs[:F_GROUPS]
    rc_ref, wf_ref, o_ref, y_ref = refs[F_GROUPS:]
    rows = rc_ref.shape[0]
    gc = wf_ref.shape[-1]
    for g in range(F_GROUPS):
        for j in range(n_cols):
            z = z_refs[g][:, :, j * gc:(j + 1) * gc].reshape(2 * rows, gc)
            y_ref[j * rows:(j + 1) * rows, :] = _dot(rc_ref[...], z).astype(BF16)
        out = _dot(y_ref[...], wf_ref[g]).astype(BF16)
        for j in range(n_cols):
            lane0 = (j * F_GROUPS + g) * gc
            o_ref[:, lane0:lane0 + gc] = out[j * rows:(j + 1) * rows, :]


def _fourier_lat2(zp5, rc, wf, gc, n_cols=16):
    bsz, _, _, rows, lanes = zp5.shape
    cols = lanes // gc

    def z_spec(g):
        return pl.BlockSpec((None, None, 2, rows, n_cols * gc), lambda b, cb: (b, g, 0, 0, cb))

    return pl.pallas_call(
        functools.partial(_fourier_lat2_kernel, n_cols=n_cols),
        grid=(bsz, cols // n_cols),
        in_specs=[z_spec(g) for g in range(F_GROUPS)] + [
            pl.BlockSpec((rows, 2 * rows), lambda b, cb: (0, 0)),
            pl.BlockSpec((F_GROUPS, gc, gc), lambda b, cb: (0, 0, 0)),
        ],
        out_specs=pl.BlockSpec((None, rows, n_cols * F_GROUPS * gc), lambda b, cb: (b, 0, cb)),
        out_shape=jax.ShapeDtypeStruct((bsz, rows, cols * F_GROUPS * gc), BF16),
        scratch_shapes=[pltpu.VMEM((n_cols * rows, gc), BF16)],
        compiler_params=_params("parallel", "parallel"),
        name="fourier_lat2",
    )(*([zp5] * F_GROUPS), rc, wf)


def _outproj_kernel(f_ref, fg_ref, g_ref, x_ref, mod_ref, fnw_ref, w_ref, o_ref, lhs_ref, *, n_j):
    tm, fw = fg_ref.shape
    tn = w_ref.shape[-1]
    row_chunk = 128
    j = pl.program_id(1)

    @pl.when(j == 0)
    def _():
        fg = fg_ref[...].astype(F32)
        lhs_ref[:, :fw] = (f_ref[...].astype(F32) * (fg * jax.nn.sigmoid(fg))).astype(BF16)
        lhs_ref[:, fw:] = g_ref[...]

    pre = x_ref[...] + mod_ref[2:3, :] * _dot(lhs_ref[...], w_ref[...])
    for jj in range(n_j):
        @pl.when(j == jj)
        def _(jj=jj):
            o_ref[:, jj * tn:(jj + 1) * tn] = pre

    @pl.when(j == n_j - 1)
    def _():
        fnw = fnw_ref[...]

        def body(r, carry):
            rs = pl.ds(pl.multiple_of(r * row_chunk, row_chunk), row_chunk)
            y = o_ref[rs, :]
            ms = jnp.mean(y * y, axis=-1, keepdims=True)
            o_ref[rs, :] = (y * lax.rsqrt(ms + EPS)) * fnw
            return carry

        lax.fori_loop(0, tm // row_chunk, body, 0)


def _outproj(f2, p2, g2, x2, mod3, fnw, w_bf, mod_row, name, tm=512, tn=512):
    t, d = x2.shape
    fw = g2.shape[-1]
    n_j = d // tn
    return pl.pallas_call(
        functools.partial(_outproj_kernel, n_j=n_j),
        grid=(t // tm, n_j),
        in_specs=[
            pl.BlockSpec((tm, fw), lambda i, j: (i, 0)),
            pl.BlockSpec((tm, fw), lambda i, j: (i, 1)),
            pl.BlockSpec((tm, fw), lambda i, j: (i, 0)),
            pl.BlockSpec((tm, tn), lambda i, j: (i, j)),
            pl.BlockSpec((None, 3, tn), lambda i, j: (mod_row(i), 0, j)),
            pl.BlockSpec((1, d), lambda i, j: (0, 0)),
            pl.BlockSpec((d, tn), lambda i, j: (0, j)),
        ],
        out_specs=pl.BlockSpec((tm, d), lambda i, j: (i, 0)),
        out_shape=jax.ShapeDtypeStruct((t, d), F32),
        scratch_shapes=[pltpu.VMEM((tm, d), BF16)],
        compiler_params=_params("parallel", "arbitrary"),
        name=name,
    )(f2, p2, g2, x2, mod3, fnw, w_bf)


def kernel(x_prompt, x_sample, state_gla_fwd, state_gla_bwd, c, c_ctx, ada_w, ada_b, norm_w,
           w_in, w_alpha, b_alpha, w_fourier, gla_norm_w, w_out, final_norm_w):
    bp, lp, d = x_prompt.shape
    bs, ls, _ = x_sample.shape
    depth = ada_w.shape[0]
    assert depth == 1, "single-layer step"
    fw = d // 2
    gc = fw // F_GROUPS
    dk_total = w_alpha.shape[-1]
    n_main = 2 * fw + 2 * dk_total + 2 * (d - fw)
    gla_offsets = (2 * fw, 2 * fw + dk_total, 2 * fw + 2 * dk_total, 2 * fw + 2 * dk_total + (d - fw))
    rows = ls // GRID_W
    assert bs + 1 <= MOD_ROWS and rows == GRID_W and 2 * GLA_LOWRANK <= LANES

    cc = jnp.zeros((MOD_ROWS, d), F32).at[:bs].set(c).at[bs].set(c_ctx)
    mod3 = _modulation(cc, ada_w[0], ada_b).reshape(MOD_ROWS, 3, d)

    w_main = w_in[0, :, :n_main].astype(BF16)
    w_a = jnp.zeros((d, LANES), BF16).at[:, :2 * GLA_LOWRANK].set(w_in[0, :, n_main:].astype(BF16))
    w_out_bf = w_out[0].astype(BF16)
    wf_bf = w_fourier[0].astype(BF16)
    nw = norm_w[0][None, :]
    fnw = final_norm_w[None, :]
    wal = jnp.zeros((2, LANES, dk_total), F32)
    wal = wal.at[0, :GLA_LOWRANK].set(w_alpha[0, 0]).at[1, GLA_LOWRANK:2 * GLA_LOWRANK].set(w_alpha[0, 1])
    wal_hi = wal.astype(BF16)
    wal_lo = (wal - wal_hi.astype(F32)).astype(BF16)
    bal = b_alpha[0][:, None, :]
    gnw = gla_norm_w[0][:, None, :]

    cc_c, cc_s = _dft_mats(gc)
    cs = jnp.asarray(np.concatenate([cc_c, cc_s], axis=1)).astype(BF16)
    csn = jnp.asarray(np.concatenate([cc_c, -cc_s], axis=1)).astype(BF16)
    cl_c, cl_s = _dft_mats(lp)
    cl = jnp.asarray(cl_c).astype(BF16)
    sl = jnp.asarray(cl_s).astype(BF16)
    g_c, g_s = _dft_mats(GRID_W)
    eye2 = np.eye(2, dtype=np.float32)
    a2, b2 = np.kron(eye2, g_c), np.kron(eye2, g_s)
    m2 = jnp.asarray(np.block([[a2, b2], [-b2, a2]])).astype(BF16)
    rc = jnp.asarray(np.concatenate([g_c, g_s], axis=1)).astype(BF16)

    xp2 = x_prompt.reshape(bp * lp, d)
    ctx_row = lambda i: bs
    p_ctx, a_ctx = _inproj(xp2, mod3, nw, w_main, w_a, ctx_row, "inproj_ctx")
    p3 = p_ctx.reshape(bp, lp, n_main)
    g_ctx, new_f, new_b = _gla(p3, a_ctx.reshape(bp, lp, LANES), wal_hi, wal_lo, bal, gnw,
                               None, None, True, gla_offsets, "gla_ctx")
    f_ctx = _fourier_ctx(p3, cs, cl, sl, wf_bf)
    y_prompt = _outproj(f_ctx.reshape(bp * lp, fw), p_ctx, g_ctx.reshape(bp * lp, d - fw),
                        xp2, mod3, fnw, w_out_bf, ctx_row, "outproj_ctx")

    xs2 = x_sample.reshape(bs * ls, d)
    tm_i = 1024
    p_lat, a_lat = _inproj(xs2, mod3, nw, w_main, w_a, lambda i: i // (ls // tm_i), "inproj_lat", tm=tm_i)
    q3 = p_lat.reshape(bs, ls, n_main)
    (g_lat,) = _gla(q3, a_lat.reshape(bs, ls, LANES), wal_hi, wal_lo, bal, gnw,
                    state_gla_fwd, state_gla_bwd, False, gla_offsets, "gla_lat")
    zp = _fourier_lat1(q3, csn, m2, gc)
    f_lat = _fourier_lat2(zp.reshape(bs, F_GROUPS, 2, rows, GRID_W * gc), rc, wf_bf, gc)
    tm_o = 512
    y_sample = _outproj(f_lat.reshape(bs * ls, fw), p_lat, g_lat.reshape(bs * ls, d - fw), xs2, mod3, fnw,
                        w_out_bf, lambda i: i // (ls // tm_o), "outproj_lat", tm=tm_o)

    return (y_prompt.reshape(bp, lp, d), y_sample.reshape(bs, ls, d), new_f, new_b)
```

```python
import functools

import numpy as np
import jax
import jax.numpy as jnp
from jax import lax
from jax.experimental import pallas as pl
from jax.experimental.pallas import tpu as pltpu

F32 = jnp.float32
BF16 = jnp.bfloat16

GRID_W = 64
F_GROUPS = 4
GLA_HEADS = 4
GLA_LOWRANK = 16
GLA_TAU = 16.0
GLA_CHUNK = 64
GLA_BLOCK = 256
GLA_SLOTS = 4
EPS = 1e-6

VMEM_LIMIT_BYTES = 56 * 1024 * 1024
LANES = 128
MOD_ROWS = 16


def _params(*sem):
    return pltpu.CompilerParams(dimension_semantics=sem, vmem_limit_bytes=VMEM_LIMIT_BYTES)


def _dot(a, b):
    return jnp.dot(a, b, preferred_element_type=F32)


def _split2(x):
    hi = x.astype(BF16)
    lo = (x - hi.astype(F32)).astype(BF16)
    return hi, lo


def _dft_mats(n):
    k = np.arange(n)
    ang = 2.0 * np.pi * ((k[:, None] * k[None, :]) % n) / n
    s = 1.0 / np.sqrt(n)
    return (np.cos(ang) * s).astype(np.float32), (np.sin(ang) * s).astype(np.float32)


def _mod_kernel(c_ref, w_ref, b_ref, o_ref):
    c = c_ref[...]
    s = c * jax.nn.sigmoid(c)
    s_hi, s_lo = _split2(s)
    w_hi, w_lo = _split2(w_ref[...])
    o_ref[...] = _dot(s_hi, w_hi) + _dot(s_hi, w_lo) + _dot(s_lo, w_hi) + b_ref[...]


def _modulation(cc, ada_w, ada_b, tn=512):
    d, n = ada_w.shape
    return pl.pallas_call(
        _mod_kernel,
        grid=(n // tn,),
        in_specs=[
            pl.BlockSpec((MOD_ROWS, d), lambda j: (0, 0)),
            pl.BlockSpec((d, tn), lambda j: (0, j)),
            pl.BlockSpec((1, tn), lambda j: (0, j)),
        ],
        out_specs=pl.BlockSpec((MOD_ROWS, tn), lambda j: (0, j)),
        out_shape=jax.ShapeDtypeStruct((MOD_ROWS, n), F32),
        compiler_params=_params("parallel"),
        name="mod",
    )(cc, ada_w, ada_b)


def _inproj_kernel(x_ref, mod_ref, nw_ref, w_ref, wa_ref, p_ref, a_ref, h_ref, *, row_chunk):
    tm = x_ref.shape[0]

    @pl.when(pl.program_id(1) == 0)
    def _():
        sh = mod_ref[0:1, :]
        sc1 = 1.0 + mod_ref[1:2, :]
        nw = nw_ref[...]

        def body(r, carry):
            rs = pl.ds(pl.multiple_of(r * row_chunk, row_chunk), row_chunk)
            x = x_ref[rs, :]
            ms = jnp.mean(x * x, axis=-1, keepdims=True)
            y = (x * lax.rsqrt(ms + EPS)) * nw
            h_ref[rs, :] = (y * sc1 + sh).astype(BF16)
            return carry

        lax.fori_loop(0, tm // row_chunk, body, 0)
        a_ref[...] = _dot(h_ref[...], wa_ref[...])

    p_ref[...] = _dot(h_ref[...], w_ref[...]).astype(BF16)


def _inproj(x2, mod3, nw, w_bf, wa_bf, mod_row, name, tm=1024, tn=512):
    t, d = x2.shape
    n = w_bf.shape[1]
    return pl.pallas_call(
        functools.partial(_inproj_kernel, row_chunk=128),
        grid=(t // tm, n // tn),
        in_specs=[
            pl.BlockSpec((tm, d), lambda i, j: (i, 0)),
            pl.BlockSpec((None, 3, d), lambda i, j: (mod_row(i), 0, 0)),
            pl.BlockSpec((1, d), lambda i, j: (0, 0)),
            pl.BlockSpec((d, tn), lambda i, j: (0, j)),
            pl.BlockSpec((d, LANES), lambda i, j: (0, 0)),
        ],
        out_specs=[
            pl.BlockSpec((tm, tn), lambda i, j: (i, j)),
            pl.BlockSpec((tm, LANES), lambda i, j: (i, 0)),
        ],
        out_shape=[
            jax.ShapeDtypeStruct((t, n), BF16),
            jax.ShapeDtypeStruct((t, LANES), F32),
        ],
        scratch_shapes=[pltpu.VMEM((tm, d), BF16)],
        compiler_params=_params("parallel", "arbitrary"),
        name=name,
    )(x2, mod3, nw, w_bf, wa_bf)


def _gla_kernel(*refs, seq_len, has_state, emit_state):
    q_ref, k_ref, v_ref, gg_ref, a_ref, wz_ref, bal_ref, gnw_ref = refs[:8]
    pos = 8
    if has_state:
        s0f_ref, s0b_ref = refs[pos:pos + 2]
        pos += 2
    o_ref = refs[pos]
    pos += 1
    if emit_state:
        sf_ref, sb_ref = refs[pos:pos + 2]
        pos += 2
    of_ref, s_ref, b_scr, dcol_scr, qh_scr, kh_scr = refs[pos:pos + 6]

    c = GLA_CHUNK
    n_sub = GLA_BLOCK // c
    n_blocks = seq_len // GLA_BLOCK
    dk = q_ref.shape[-1]
    q_scale = float(dk) ** -0.5
    lrow = lax.broadcasted_iota(jnp.int32, (c, GLA_BLOCK), 0)
    bcol = lax.broadcasted_iota(jnp.int32, (c, GLA_BLOCK), 1)
    ccol = bcol // c
    lcol = bcol - ccol * c
    a_group = lax.broadcasted_iota(jnp.int32, (GLA_BLOCK, LANES), 1) // (2 * GLA_LOWRANK)
    contract_last = (((1,), (1,)), ((), ()))
    contract_first = (((0,), (0,)), ((), ()))
    sub = lambda x, j: x[j * c:(j + 1) * c]

    def rows_of(blk):
        start = blk * GLA_BLOCK
        if not isinstance(blk, int):
            start = pl.multiple_of(start, GLA_BLOCK)
        return pl.ds(start, GLA_BLOCK)

    def slot_of(blk):
        return blk % GLA_SLOTS

    def chunk_masks(fwd):
        return [(ccol == j) & ((lrow >= lcol) if fwd else (lcol >= lrow)) for j in range(n_sub)]

    def stage_decay(blk, direction):
        fwd = direction == 0
        a = a_ref[rows_of(blk), :]
        hi = a.astype(BF16).astype(F32)
        lhs = jnp.where(a_group == 1, a - hi, jnp.where(a_group == 3, 0.0, hi)).astype(BF16)
        z = _dot(lhs, wz_ref[direction]) + bal_ref[direction]
        g = (jnp.minimum(z, 0.0) - jnp.log(1.0 + jnp.exp(-jnp.abs(z)))) * (1.0 / GLA_TAU)
        tri = jnp.concatenate(chunk_masks(fwd), axis=0).astype(BF16)
        g_hi, g_lo = _split2(g)
        b_scr[slot_of(blk)] = _dot(tri, g_hi) + _dot(tri, g_lo)
        dcol_scr[slot_of(blk)] = jnp.exp(jnp.sum(g.T, axis=-1, keepdims=True))

    def stage_intra(blk, direction):
        fwd = direction == 0
        rs = rows_of(blk)
        scan = list(range(n_sub)) if fwd else list(range(n_sub - 1, -1, -1))
        at = {j: p for p, j in enumerate(scan)}
        masks0 = chunk_masks(fwd)
        b = b_scr[slot_of(blk)]
        t = [sub(b, j)[c - 1:c, :] if fwd else sub(b, j)[0:1, :] for j in range(n_sub)]
        b_end = jnp.concatenate([jnp.broadcast_to(t[j], (c, dk)) for j in range(n_sub)], axis=0)
        q = q_ref[rs, :].astype(F32)
        k = k_ref[rs, :].astype(F32)
        q32 = q * q_scale * jnp.exp(b)
        q_t = q32.astype(BF16)
        k_t = (k * jnp.exp(-b)).astype(BF16)
        kd32 = k * jnp.exp(b_end - b)
        k_dec = kd32.astype(BF16)

        before, acc = {}, jnp.zeros_like(t[0])
        for j in scan:
            before[j] = acc
            acc = acc + t[j]
        total = acc
        qh_scr[slot_of(blk)] = jnp.concatenate(
            [sub(q32, j) * jnp.exp(before[j]) for j in range(n_sub)], axis=0).astype(BF16)
        kh_scr[slot_of(blk)] = jnp.concatenate(
            [sub(kd32, j) * jnp.exp(total - before[j] - t[j]) for j in range(n_sub)], axis=0).astype(BF16)

        lhs, where = [q_t], {}
        n_rows = GLA_BLOCK
        for d in range(2, n_sub):
            for j in scan[d:]:
                between = scan[at[j] - d + 1:at[j]]
                gap = t[between[0]]
                for i in between[1:]:
                    gap = gap + t[i]
                lhs.append((sub(q32, j) * jnp.exp(gap)).astype(BF16))
                where[(j, d)] = n_rows
                n_rows += c
        r = lax.dot_general(jnp.concatenate(lhs, axis=0), k_dec, contract_last, preferred_element_type=F32)
        a0 = lax.dot_general(q_t, k_t, contract_last, preferred_element_type=F32)
        att_rows = []
        for j in range(n_sub):
            att = jnp.where(masks0[j], sub(a0, j), 0.0)
            for d in range(1, at[j] + 1):
                src = sub(r, j) if d == 1 else r[where[(j, d)]:where[(j, d)] + c]
                att = jnp.where(ccol == scan[at[j] - d], src, att)
            att_rows.append(att)
        att = jnp.concatenate(att_rows, axis=0).astype(BF16)
        o_intra = _dot(att, v_ref[rs, :])
        if fwd:
            of_ref[rs, :] = o_intra
        else:
            of_ref[rs, :] += o_intra

    def stage_state(blk, direction):
        rs = rows_of(blk)
        s = s_ref[direction]
        o_inter = _dot(qh_scr[slot_of(blk)], s.astype(BF16))
        upd = lax.dot_general(kh_scr[slot_of(blk)], v_ref[rs, :], contract_first, preferred_element_type=F32)
        s_ref[direction] = dcol_scr[slot_of(blk)] * s + upd
        if direction == 0:
            of_ref[rs, :] += o_inter
        else:
            o = of_ref[rs, :] + o_inter
            ms = jnp.mean(o * o, axis=-1, keepdims=True)
            y = (o * lax.rsqrt(ms + EPS)) * gnw_ref[...]
            gg = gg_ref[rs, :].astype(F32)
            o_ref[rs, :] = (y * (gg * jax.nn.sigmoid(gg))).astype(BF16)

    def run(direction):
        n = n_blocks
        order = (lambda i: i) if direction == 0 else (lambda i: n - 1 - i)
        if n == 1:
            stage_decay(0, direction)
            stage_intra(0, direction)
            stage_state(0, direction)
            return
        stage_decay(order(0), direction)
        stage_decay(order(1), direction)
        stage_intra(order(0), direction)
        if n > 2:
            def body(i, carry):
                stage_state(order(i), direction)
                stage_intra(order(i + 1), direction)
                stage_decay(order(i + 2), direction)
                return carry

            lax.fori_loop(0, n - 2, body, 0)
        stage_state(order(n - 2), direction)
        stage_intra(order(n - 1), direction)
        stage_state(order(n - 1), direction)

    if has_state:
        s_ref[0] = s0f_ref[...]
        s_ref[1] = s0b_ref[...]
    else:
        s_ref[...] = jnp.zeros_like(s_ref)
    run(0)
    if emit_state:
        sf_ref[...] = s_ref[0]
    run(1)
    if emit_state:
        sb_ref[...] = s_ref[1]


def _gla(p3, a3, wz, bal, gnw, s0f, s0b, emit_state, offsets, name):
    bsz, seq_len, _ = p3.shape
    dk = wz.shape[-1] // GLA_HEADS
    dv = gnw.shape[-1]
    off_q, off_k, off_v, off_gg = offsets
    assert off_q % dk == 0 and off_k % dk == 0 and off_v % dv == 0 and off_gg % dv == 0
    assert seq_len % GLA_BLOCK == 0 and GLA_BLOCK % GLA_CHUNK == 0
    qb, kb, vb, ggb = off_q // dk, off_k // dk, off_v // dv, off_gg // dv
    has_state = s0f is not None
    in_specs = [
        pl.BlockSpec((None, seq_len, dk), lambda b, h: (b, 0, qb + h)),
        pl.BlockSpec((None, seq_len, dk), lambda b, h: (b, 0, kb + h)),
        pl.BlockSpec((None, seq_len, dv), lambda b, h: (b, 0, vb + h)),
        pl.BlockSpec((None, seq_len, dv), lambda b, h: (b, 0, ggb + h)),
        pl.BlockSpec((None, seq_len, LANES), lambda b, h: (b, 0, 0)),
        pl.BlockSpec((2, LANES, dk), lambda b, h: (0, 0, h)),
        pl.BlockSpec((2, 1, dk), lambda b, h: (0, 0, h)),
        pl.BlockSpec((None, 1, dv), lambda b, h: (h, 0, 0)),
    ]
    args = [p3, p3, p3, p3, a3, wz, bal, gnw]
    state_spec = pl.BlockSpec((None, None, None, dk, dv), lambda b, h: (b, 0, h, 0, 0))
    if has_state:
        in_specs += [state_spec, state_spec]
        args += [s0f, s0b]
    out_specs = [pl.BlockSpec((None, seq_len, dv), lambda b, h: (b, 0, h))]
    out_shape = [jax.ShapeDtypeStruct((bsz, seq_len, GLA_HEADS * dv), BF16)]
    if emit_state:
        out_specs += [state_spec, state_spec]
        out_shape += [jax.ShapeDtypeStruct((bsz, 1, GLA_HEADS, dk, dv), F32)] * 2
    return pl.pallas_call(
        functools.partial(_gla_kernel, seq_len=seq_len, has_state=has_state, emit_state=emit_state),
        grid=(bsz, GLA_HEADS),
        in_specs=in_specs,
        out_specs=out_specs,
        out_shape=out_shape,
        scratch_shapes=[
            pltpu.VMEM((seq_len, dv), F32),
            pltpu.VMEM((2, dk, dv), F32),
            pltpu.VMEM((GLA_SLOTS, GLA_BLOCK, dk), F32),
            pltpu.VMEM((GLA_SLOTS, dk, 1), F32),
            pltpu.VMEM((GLA_SLOTS, GLA_BLOCK, dk), BF16),
            pltpu.VMEM((GLA_SLOTS, GLA_BLOCK, dk), BF16),
        ],
        compiler_params=_params("parallel", "parallel"),
        name=name,
    )(*args)


def _fourier_ctx_kernel(u_ref, fg_ref, cs_ref, cl_ref, sl_ref, wf_ref, o_ref):
    gc = u_ref.shape[-1]
    res = _dot(u_ref[...], cs_ref[...])
    uc = res[:, :gc].astype(BF16)
    us = res[:, gc:].astype(BF16)
    re = _dot(cl_ref[...], uc) - _dot(sl_ref[...], us)
    fg = fg_ref[...].astype(F32)
    o_ref[...] = (_dot(re.astype(BF16), wf_ref[...]) * (fg * jax.nn.sigmoid(fg))).astype(BF16)


def _fourier_ctx(p3, cs, cl, sl, wf):
    bsz, seq_len, _ = p3.shape
    gc = wf.shape[-1]
    return pl.pallas_call(
        _fourier_ctx_kernel,
        grid=(bsz, F_GROUPS),
        in_specs=[
            pl.BlockSpec((None, seq_len, gc), lambda b, g: (b, 0, g)),
            pl.BlockSpec((None, seq_len, gc), lambda b, g: (b, 0, F_GROUPS + g)),
            pl.BlockSpec((gc, 2 * gc), lambda b, g: (0, 0)),
            pl.BlockSpec((seq_len, seq_len), lambda b, g: (0, 0)),
            pl.BlockSpec((seq_len, seq_len), lambda b, g: (0, 0)),
            pl.BlockSpec((None, gc, gc), lambda b, g: (g, 0, 0)),
        ],
        out_specs=pl.BlockSpec((None, seq_len, gc), lambda b, g: (b, 0, g)),
        out_shape=jax.ShapeDtypeStruct((bsz, seq_len, F_GROUPS * gc), BF16),
        compiler_params=_params("parallel", "parallel"),
        name="fourier_ctx",
    )(p3, p3, cs, cl, sl, wf)


COL_TILE = 16


def _fourier_lat1_kernel(u_ref, rm_ref, y_ref):
    rows, cw, gc = u_ref.shape
    u = u_ref[...].reshape(rows * cw, gc)
    res = _dot(rm_ref[...], u).astype(BF16)
    y_ref[...] = res.reshape(2, rows, cw, gc)


def _fourier_lat1(p4, rm, gc):
    bsz, rows, cols, _ = p4.shape
    return pl.pallas_call(
        _fourier_lat1_kernel,
        grid=(bsz, cols // COL_TILE, F_GROUPS),
        in_specs=[
            pl.BlockSpec((None, rows, COL_TILE, gc), lambda b, cb, g: (b, 0, cb, g)),
            pl.BlockSpec((2 * rows * COL_TILE, rows * COL_TILE), lambda b, cb, g: (0, 0)),
        ],
        out_specs=pl.BlockSpec((None, None, 2, rows, COL_TILE, gc), lambda b, cb, g: (b, g, 0, 0, cb, 0)),
        out_shape=jax.ShapeDtypeStruct((bsz, F_GROUPS, 2, rows, cols, gc), BF16),
        compiler_params=_params("parallel", "parallel", "parallel"),
        name="fourier_lat1",
    )(p4, rm)


ROW_TILE = 16


def _fourier_lat2_kernel(y_ref, fg_ref, m2_ref, cs_ref, wf_ref, o_ref, zz_ref):
    _, rt, cols, gc = y_ref.shape
    sub = 2 * cols
    for s in range(rt // 2):
        rs = slice(s * sub, (s + 1) * sub)
        st = jnp.concatenate([y_ref[0, 2 * s:2 * s + 2].reshape(sub, gc),
                              y_ref[1, 2 * s:2 * s + 2].reshape(sub, gc)], axis=0)
        out = _dot(m2_ref[...], st)
        zz_ref[rs, :gc] = out[:sub].astype(BF16)
        zz_ref[rs, gc:] = out[sub:].astype(BF16)
    re = _dot(zz_ref[...], cs_ref[...])
    fg = fg_ref[...].astype(F32)
    o_ref[...] = (_dot(re.astype(BF16), wf_ref[...]) * (fg * jax.nn.sigmoid(fg))).astype(BF16)


def _fourier_lat2(y6, p2, m2, cs_stack, wf):
    bsz, _, _, rows, cols, gc = y6.shape
    tokens = ROW_TILE * cols
    tiles = rows // ROW_TILE
    return pl.pallas_call(
        _fourier_lat2_kernel,
        grid=(bsz, tiles, F_GROUPS),
        in_specs=[
            pl.BlockSpec((None, None, 2, ROW_TILE, cols, gc), lambda b, rb, g: (b, g, 0, rb, 0, 0)),
            pl.BlockSpec((tokens, gc), lambda b, rb, g: (b * tiles + rb, F_GROUPS + g)),
            pl.BlockSpec((4 * cols, 4 * cols), lambda b, rb, g: (0, 0)),
            pl.BlockSpec((2 * gc, gc), lambda b, rb, g: (0, 0)),
            pl.BlockSpec((None, gc, gc), lambda b, rb, g: (g, 0, 0)),
        ],
        out_specs=pl.BlockSpec((tokens, gc), lambda b, rb, g: (b * tiles + rb, g)),
        out_shape=jax.ShapeDtypeStruct((bsz * rows * cols, F_GROUPS * gc), BF16),
        scratch_shapes=[pltpu.VMEM((tokens, 2 * gc), BF16)],
        compiler_params=_params("parallel", "parallel", "parallel"),
        name="fourier_lat2",
    )(y6, p2, m2, cs_stack, wf)


def _outproj_kernel(f_ref, g_ref, x_ref, mod_ref, fnw_ref, w_ref, o_ref, *, n_j):
    tm, fw = f_ref.shape
    tn = w_ref.shape[-1]
    row_chunk = 128
    j = pl.program_id(1)
    acc = _dot(f_ref[...], w_ref[:fw, :]) + _dot(g_ref[...], w_ref[fw:, :])
    pre = x_ref[...] + mod_ref[2:3, :] * acc
    for jj in range(n_j):
        @pl.when(j == jj)
        def _(jj=jj):
            o_ref[:, jj * tn:(jj + 1) * tn] = pre

    @pl.when(j == n_j - 1)
    def _():
        fnw = fnw_ref[...]

        def body(r, carry):
            rs = pl.ds(pl.multiple_of(r * row_chunk, row_chunk), row_chunk)
            y = o_ref[rs, :]
            ms = jnp.mean(y * y, axis=-1, keepdims=True)
            o_ref[rs, :] = (y * lax.rsqrt(ms + EPS)) * fnw
            return carry

        lax.fori_loop(0, tm // row_chunk, body, 0)


def _outproj(f2, g2, x2, mod3, fnw, w_bf, mod_row, name, tm=512, tn=512):
    t, d = x2.shape
    fw = f2.shape[-1]
    n_j = d // tn
    return pl.pallas_call(
        functools.partial(_outproj_kernel, n_j=n_j),
        grid=(t // tm, n_j),
        in_specs=[
            pl.BlockSpec((tm, fw), lambda i, j: (i, 0)),
            pl.BlockSpec((tm, d - fw), lambda i, j: (i, 0)),
            pl.BlockSpec((tm, tn), lambda i, j: (i, j)),
            pl.BlockSpec((None, 3, tn), lambda i, j: (mod_row(i), 0, j)),
            pl.BlockSpec((1, d), lambda i, j: (0, 0)),
            pl.BlockSpec((d, tn), lambda i, j: (0, j)),
        ],
        out_specs=pl.BlockSpec((tm, d), lambda i, j: (i, 0)),
        out_shape=jax.ShapeDtypeStruct((t, d), F32),
        compiler_params=_params("parallel", "arbitrary"),
        name=name,
    )(f2, g2, x2, mod3, fnw, w_bf)


def kernel(x_prompt, x_sample, state_gla_fwd, state_gla_bwd, c, c_ctx, ada_w, ada_b, norm_w,
           w_in, w_alpha, b_alpha, w_fourier, gla_norm_w, w_out, final_norm_w):
    bp, lp, d = x_prompt.shape
    bs, ls, _ = x_sample.shape
    depth = ada_w.shape[0]
    assert depth == 1, "single-layer step"
    fw = d // 2
    gc = fw // F_GROUPS
    dk_total = w_alpha.shape[-1]
    n_main = 2 * fw + 2 * dk_total + 2 * (d - fw)
    gla_offsets = (2 * fw, 2 * fw + dk_total, 2 * fw + 2 * dk_total, 2 * fw + 2 * dk_total + (d - fw))
    rows = ls // GRID_W
    assert bs + 1 <= MOD_ROWS and rows == GRID_W and 8 * GLA_LOWRANK == LANES

    cc = jnp.zeros((MOD_ROWS, d), F32).at[:bs].set(c).at[bs].set(c_ctx)
    mod3 = _modulation(cc, ada_w[0], ada_b).reshape(MOD_ROWS, 3, d)

    w_main = w_in[0, :, :n_main].astype(BF16)
    w_a = jnp.tile(w_in[0, :, n_main:], (1, LANES // (2 * GLA_LOWRANK))).astype(BF16)
    w_out_bf = w_out[0].astype(BF16)
    wf_bf = w_fourier[0].astype(BF16)
    nw = norm_w[0][None, :]
    fnw = final_norm_w[None, :]
    lr = GLA_LOWRANK
    wdir = jnp.zeros((2, 2 * lr, dk_total), F32).at[0, :lr].set(w_alpha[0, 0]).at[1, lr:].set(w_alpha[0, 1])
    wdir_hi = wdir.astype(BF16)
    wdir_lo = (wdir - wdir_hi.astype(F32)).astype(BF16)
    wz = jnp.concatenate([wdir_hi, wdir_hi, wdir_lo, jnp.zeros_like(wdir_hi)], axis=1)
    bal = b_alpha[0][:, None, :]
    gnw = gla_norm_w[0][:, None, :]

    cc_c, cc_s = _dft_mats(gc)
    cs = jnp.asarray(np.concatenate([cc_c, cc_s], axis=1)).astype(BF16)
    cs_stack = jnp.asarray(np.concatenate([cc_c, cc_s], axis=0)).astype(BF16)
    cl_c, cl_s = _dft_mats(lp)
    cl = jnp.asarray(cl_c).astype(BF16)
    sl = jnp.asarray(cl_s).astype(BF16)
    g_c, g_s = _dft_mats(GRID_W)
    eye2 = np.eye(2, dtype=np.float32)
    a2, b2 = np.kron(eye2, g_c), np.kron(eye2, g_s)
    m2 = jnp.asarray(np.block([[a2, b2], [-b2, a2]])).astype(BF16)
    eye_t = np.eye(COL_TILE, dtype=np.float32)
    rm = jnp.asarray(np.concatenate([np.kron(g_c, eye_t), -np.kron(g_s, eye_t)], axis=0)).astype(BF16)

    xp2 = x_prompt.reshape(bp * lp, d)
    ctx_row = lambda i: bs
    p_ctx, a_ctx = _inproj(xp2, mod3, nw, w_main, w_a, ctx_row, "inproj_ctx")
    p3 = p_ctx.reshape(bp, lp, n_main)
    g_ctx, new_f, new_b = _gla(p3, a_ctx.reshape(bp, lp, LANES), wz, bal, gnw,
                               None, None, True, gla_offsets, "gla_ctx")
    f_ctx = _fourier_ctx(p3, cs, cl, sl, wf_bf)
    y_prompt = _outproj(f_ctx.reshape(bp * lp, fw), g_ctx.reshape(bp * lp, d - fw),
                        xp2, mod3, fnw, w_out_bf, ctx_row, "outproj_ctx")

    xs2 = x_sample.reshape(bs * ls, d)
    tm_i = 1024
    p_lat, a_lat = _inproj(xs2, mod3, nw, w_main, w_a, lambda i: i // (ls // tm_i), "inproj_lat", tm=tm_i)
    (g_lat,) = _gla(p_lat.reshape(bs, ls, n_main), a_lat.reshape(bs, ls, LANES), wz, bal, gnw,
                    state_gla_fwd, state_gla_bwd, False, gla_offsets, "gla_lat")
    y6 = _fourier_lat1(p_lat.reshape(bs, rows, GRID_W, n_main), rm, gc)
    f_lat = _fourier_lat2(y6, p_lat, m2, cs_stack, wf_bf)
    tm_o = 512
    y_sample = _outproj(f_lat, g_lat.reshape(bs * ls, d - fw), xs2, mod3, fnw,
                        w_out_bf, lambda i: i // (ls // tm_o), "outproj_lat", tm=tm_o)

    return (y_prompt.reshape(bp, lp, d), y_sample.reshape(bs, ls, d), new_f, new_b)
```

```python
import functools

import numpy as np
import jax
import jax.numpy as jnp
from jax import lax
from jax.experimental import pallas as pl
from jax.experimental.pallas import tpu as pltpu

F32 = jnp.float32
BF16 = jnp.bfloat16

GRID_W = 64
F_GROUPS = 4
GLA_HEADS = 4
GLA_LOWRANK = 16
GLA_TAU = 16.0
GLA_CHUNK = 64
GLA_BLOCK = 256
GLA_SLOTS = 4
EPS = 1e-6

VMEM_LIMIT_BYTES = 58 * 1024 * 1024
LANES = 128
MOD_ROWS = 16


def _params(*sem):
    return pltpu.CompilerParams(dimension_semantics=sem, vmem_limit_bytes=VMEM_LIMIT_BYTES)


def _dot(a, b):
    return jnp.dot(a, b, preferred_element_type=F32)


def _split2(x):
    hi = x.astype(BF16)
    lo = (x - hi.astype(F32)).astype(BF16)
    return hi, lo


def _dft_mats(n):
    k = np.arange(n)
    ang = 2.0 * np.pi * ((k[:, None] * k[None, :]) % n) / n
    s = 1.0 / np.sqrt(n)
    return (np.cos(ang) * s).astype(np.float32), (np.sin(ang) * s).astype(np.float32)


def _mod_kernel(c_ref, w_ref, b_ref, o_ref):
    c = c_ref[...]
    s = c * jax.nn.sigmoid(c)
    s_hi, s_lo = _split2(s)
    w_hi, w_lo = _split2(w_ref[...])
    o_ref[...] = _dot(s_hi, w_hi) + _dot(s_hi, w_lo) + _dot(s_lo, w_hi) + b_ref[...]


def _modulation(cc, ada_w, ada_b, tn=512):
    d, n = ada_w.shape
    return pl.pallas_call(
        _mod_kernel,
        grid=(n // tn,),
        in_specs=[
            pl.BlockSpec((MOD_ROWS, d), lambda j: (0, 0)),
            pl.BlockSpec((d, tn), lambda j: (0, j)),
            pl.BlockSpec((1, tn), lambda j: (0, j)),
        ],
        out_specs=pl.BlockSpec((MOD_ROWS, tn), lambda j: (0, j)),
        out_shape=jax.ShapeDtypeStruct((MOD_ROWS, n), F32),
        compiler_params=_params("parallel"),
        name="mod",
    )(cc, ada_w, ada_b)


def _inproj_kernel(x_ref, mod_ref, nw_ref, w_ref, wa_ref, p_ref, a_ref, h0_ref, h1_ref, *, n_slices):
    i = pl.program_id(0)
    j = pl.program_id(1)
    rows = x_ref.shape[0]
    odd = i % 2

    def step(h_cur, h_prev, first):
        x = x_ref[...]
        ms = jnp.mean(x * x, axis=-1, keepdims=True)
        y = (x * lax.rsqrt(ms + EPS)) * nw_ref[...]
        h = y * (1.0 + mod_ref[1:2, :]) + mod_ref[0:1, :]
        start = pl.multiple_of(jnp.minimum(j, n_slices - 1) * rows, rows)
        h_cur[pl.ds(start, rows), :] = h.astype(BF16)
        if first:
            return
        p_ref[...] = _dot(h_prev[...], w_ref[...]).astype(BF16)

        @pl.when(j == 0)
        def _():
            a_ref[...] = _dot(h_prev[...], wa_ref[...])

    @pl.when(i == 0)
    def _():
        step(h0_ref, h1_ref, True)

    @pl.when((i > 0) & (odd == 0))
    def _():
        step(h0_ref, h1_ref, False)

    @pl.when(odd == 1)
    def _():
        step(h1_ref, h0_ref, False)


def _inproj(x2, mod3, nw, w_bf, wa_bf, n_main, mod_row, name, tm=1024, tn=512, n_slices=16):
    t, d = x2.shape
    n_i, n_j = t // tm, n_main // tn
    rows = tm // n_slices
    assert n_slices <= n_j and rows % 8 == 0
    tile = lambda i: jnp.minimum(i, n_i - 1)
    done = lambda i: jnp.maximum(i - 1, 0)
    col = lambda i, j: jnp.where(i == 0, 0, j)
    return pl.pallas_call(
        functools.partial(_inproj_kernel, n_slices=n_slices),
        grid=(n_i + 1, n_j),
        in_specs=[
            pl.BlockSpec((rows, d), lambda i, j: (tile(i) * n_slices + jnp.minimum(j, n_slices - 1), 0)),
            pl.BlockSpec((None, 3, d), lambda i, j: (mod_row(tile(i)), 0, 0)),
            pl.BlockSpec((1, d), lambda i, j: (0, 0)),
            pl.BlockSpec((d, tn), lambda i, j: (0, col(i, j))),
            pl.BlockSpec((d, LANES), lambda i, j: (0, 0)),
        ],
        out_specs=[
            pl.BlockSpec((tm, tn), lambda i, j: (done(i), col(i, j))),
            pl.BlockSpec((tm, LANES), lambda i, j: (done(i), 0)),
        ],
        out_shape=[
            jax.ShapeDtypeStruct((t, n_main), BF16),
            jax.ShapeDtypeStruct((t, LANES), F32),
        ],
        scratch_shapes=[pltpu.VMEM((tm, d), BF16), pltpu.VMEM((tm, d), BF16)],
        compiler_params=_params("arbitrary", "arbitrary"),
        name=name,
    )(x2, mod3, nw, w_bf, wa_bf)


def _gla_kernel(*refs, seq_len, has_state, emit_state):
    q_ref, k_ref, v_ref, gg_ref, a_ref, wz_ref, bal_ref, gnw_ref = refs[:8]
    pos = 8
    if has_state:
        s0f_ref, s0b_ref = refs[pos:pos + 2]
        pos += 2
    o_ref = refs[pos]
    pos += 1
    if emit_state:
        sf_ref, sb_ref = refs[pos:pos + 2]
        pos += 2
    of_ref, s_ref, b_scr, dcol_scr, qh_scr, kh_scr = refs[pos:pos + 6]

    c = GLA_CHUNK
    n_sub = GLA_BLOCK // c
    n_blocks = seq_len // GLA_BLOCK
    dk = q_ref.shape[-1]
    q_scale = float(dk) ** -0.5
    lrow = lax.broadcasted_iota(jnp.int32, (c, GLA_BLOCK), 0)
    bcol = lax.broadcasted_iota(jnp.int32, (c, GLA_BLOCK), 1)
    ccol = bcol // c
    lcol = bcol - ccol * c
    a_group = lax.broadcasted_iota(jnp.int32, (GLA_BLOCK, LANES), 1) // (2 * GLA_LOWRANK)
    contract_last = (((1,), (1,)), ((), ()))
    contract_first = (((0,), (0,)), ((), ()))
    sub = lambda x, j: x[j * c:(j + 1) * c]

    def rows_of(blk):
        start = blk * GLA_BLOCK
        if not isinstance(blk, int):
            start = pl.multiple_of(start, GLA_BLOCK)
        return pl.ds(start, GLA_BLOCK)

    def slot_of(blk):
        return blk % GLA_SLOTS

    def chunk_masks(fwd):
        return [(ccol == j) & ((lrow >= lcol) if fwd else (lcol >= lrow)) for j in range(n_sub)]

    def stage_decay(blk, direction):
        fwd = direction == 0
        a = a_ref[rows_of(blk), :]
        hi = a.astype(BF16).astype(F32)
        lhs = jnp.where(a_group == 1, a - hi, jnp.where(a_group == 3, 0.0, hi)).astype(BF16)
        z = _dot(lhs, wz_ref[direction]) + bal_ref[direction]
        g = (jnp.minimum(z, 0.0) - jnp.log(1.0 + jnp.exp(-jnp.abs(z)))) * (1.0 / GLA_TAU)
        tri = jnp.concatenate(chunk_masks(fwd), axis=0).astype(BF16)
        g_hi, g_lo = _split2(g)
        b_scr[slot_of(blk)] = _dot(tri, g_hi) + _dot(tri, g_lo)
        dcol_scr[slot_of(blk)] = jnp.exp(jnp.sum(g.T, axis=-1, keepdims=True))

    def stage_intra(blk, direction):
        fwd = direction == 0
        rs = rows_of(blk)
        scan = list(range(n_sub)) if fwd else list(range(n_sub - 1, -1, -1))
        at = {j: p for p, j in enumerate(scan)}
        masks0 = chunk_masks(fwd)
        b = b_scr[slot_of(blk)]
        t = [sub(b, j)[c - 1:c, :] if fwd else sub(b, j)[0:1, :] for j in range(n_sub)]
        b_end = jnp.concatenate([jnp.broadcast_to(t[j], (c, dk)) for j in range(n_sub)], axis=0)
        q = q_ref[rs, :].astype(F32)
        k = k_ref[rs, :].astype(F32)
        q32 = q * q_scale * jnp.exp(b)
        q_t = q32.astype(BF16)
        k_t = (k * jnp.exp(-b)).astype(BF16)
        kd32 = k * jnp.exp(b_end - b)
        k_dec = kd32.astype(BF16)

        before, acc = {}, jnp.zeros_like(t[0])
        for j in scan:
            before[j] = acc
            acc = acc + t[j]
        total = acc
        qh_scr[slot_of(blk)] = jnp.concatenate(
            [sub(q32, j) * jnp.exp(before[j]) for j in range(n_sub)], axis=0).astype(BF16)
        kh_scr[slot_of(blk)] = jnp.concatenate(
            [sub(kd32, j) * jnp.exp(total - before[j] - t[j]) for j in range(n_sub)], axis=0).astype(BF16)

        lhs, where = [q_t], {}
        n_rows = GLA_BLOCK
        for d in range(2, n_sub):
            for j in scan[d:]:
                between = scan[at[j] - d + 1:at[j]]
                gap = t[between[0]]
                for i in between[1:]:
                    gap = gap + t[i]
                lhs.append((sub(q32, j) * jnp.exp(gap)).astype(BF16))
                where[(j, d)] = n_rows
                n_rows += c
        r = lax.dot_general(jnp.concatenate(lhs, axis=0), k_dec, contract_last, preferred_element_type=F32)
        a0 = lax.dot_general(q_t, k_t, contract_last, preferred_element_type=F32)
        att_rows = []
        for j in range(n_sub):
            att = jnp.where(masks0[j], sub(a0, j), 0.0)
            for d in range(1, at[j] + 1):
                src = sub(r, j) if d == 1 else r[where[(j, d)]:where[(j, d)] + c]
                att = jnp.where(ccol == scan[at[j] - d], src, att)
            att_rows.append(att)
        att = jnp.concatenate(att_rows, axis=0).astype(BF16)
        o_intra = _dot(att, v_ref[rs, :])
        if fwd:
            of_ref[rs, :] = o_intra
        else:
            of_ref[rs, :] += o_intra

    def stage_state(blk, direction):
        rs = rows_of(blk)
        s = s_ref[direction]
        o_inter = _dot(qh_scr[slot_of(blk)], s.astype(BF16))
        upd = lax.dot_general(kh_scr[slot_of(blk)], v_ref[rs, :], contract_first, preferred_element_type=F32)
        s_ref[direction] = dcol_scr[slot_of(blk)] * s + upd
        if direction == 0:
            of_ref[rs, :] += o_inter
        else:
            o = of_ref[rs, :] + o_inter
            ms = jnp.mean(o * o, axis=-1, keepdims=True)
            y = (o * lax.rsqrt(ms + EPS)) * gnw_ref[...]
            gg = gg_ref[rs, :].astype(F32)
            o_ref[rs, :] = (y * (gg * jax.nn.sigmoid(gg))).astype(BF16)

    def run(direction):
        n = n_blocks
        order = (lambda i: i) if direction == 0 else (lambda i: n - 1 - i)
        if n == 1:
            stage_decay(0, direction)
            stage_intra(0, direction)
            stage_state(0, direction)
            return
        stage_decay(order(0), direction)
        stage_decay(order(1), direction)
        stage_intra(order(0), direction)
        if n > 2:
            def body(i, carry):
                stage_state(order(i), direction)
                stage_intra(order(i + 1), direction)
                stage_decay(order(i + 2), direction)
                return carry

            lax.fori_loop(0, n - 2, body, 0)
        stage_state(order(n - 2), direction)
        stage_intra(order(n - 1), direction)
        stage_state(order(n - 1), direction)

    if has_state:
        s_ref[0] = s0f_ref[...]
        s_ref[1] = s0b_ref[...]
    else:
        s_ref[...] = jnp.zeros_like(s_ref)
    run(0)
    if emit_state:
        sf_ref[...] = s_ref[0]
    run(1)
    if emit_state:
        sb_ref[...] = s_ref[1]


def _gla(p3, a3, wz, bal, gnw, s0f, s0b, emit_state, offsets, name):
    bsz, seq_len, _ = p3.shape
    dk = wz.shape[-1] // GLA_HEADS
    dv = gnw.shape[-1]
    off_q, off_k, off_v, off_gg = offsets
    assert off_q % dk == 0 and off_k % dk == 0 and off_v % dv == 0 and off_gg % dv == 0
    assert seq_len % GLA_BLOCK == 0 and GLA_BLOCK % GLA_CHUNK == 0
    qb, kb, vb, ggb = off_q // dk, off_k // dk, off_v // dv, off_gg // dv
    has_state = s0f is not None
    in_specs = [
        pl.BlockSpec((None, seq_len, dk), lambda b, h: (b, 0, qb + h)),
        pl.BlockSpec((None, seq_len, dk), lambda b, h: (b, 0, kb + h)),
        pl.BlockSpec((None, seq_len, dv), lambda b, h: (b, 0, vb + h)),
        pl.BlockSpec((None, seq_len, dv), lambda b, h: (b, 0, ggb + h)),
        pl.BlockSpec((None, seq_len, LANES), lambda b, h: (b, 0, 0)),
        pl.BlockSpec((2, LANES, dk), lambda b, h: (0, 0, h)),
        pl.BlockSpec((2, 1, dk), lambda b, h: (0, 0, h)),
        pl.BlockSpec((None, 1, dv), lambda b, h: (h, 0, 0)),
    ]
    args = [p3, p3, p3, p3, a3, wz, bal, gnw]
    state_spec = pl.BlockSpec((None, None, None, dk, dv), lambda b, h: (b, 0, h, 0, 0))
    if has_state:
        in_specs += [state_spec, state_spec]
        args += [s0f, s0b]
    out_specs = [pl.BlockSpec((None, seq_len, dv), lambda b, h: (b, 0, h))]
    out_shape = [jax.ShapeDtypeStruct((bsz, seq_len, GLA_HEADS * dv), BF16)]
    if emit_state:
        out_specs += [state_spec, state_spec]
        out_shape += [jax.ShapeDtypeStruct((bsz, 1, GLA_HEADS, dk, dv), F32)] * 2
    return pl.pallas_call(
        functools.partial(_gla_kernel, seq_len=seq_len, has_state=has_state, emit_state=emit_state),
        grid=(bsz, GLA_HEADS),
        in_specs=in_specs,
        out_specs=out_specs,
        out_shape=out_shape,
        scratch_shapes=[
            pltpu.VMEM((seq_len, dv), F32),
            pltpu.VMEM((2, dk, dv), F32),
            pltpu.VMEM((GLA_SLOTS, GLA_BLOCK, dk), F32),
            pltpu.VMEM((GLA_SLOTS, dk, 1), F32),
            pltpu.VMEM((GLA_SLOTS, GLA_BLOCK, dk), BF16),
            pltpu.VMEM((GLA_SLOTS, GLA_BLOCK, dk), BF16),
        ],
        compiler_params=_params("parallel", "parallel"),
        name=name,
    )(*args)


def _fourier_ctx_kernel(u_ref, fg_ref, cs_ref, cl_ref, sl_ref, wf_ref, o_ref):
    gc = u_ref.shape[-1]
    res = _dot(u_ref[...], cs_ref[...])
    uc = res[:, :gc].astype(BF16)
    us = res[:, gc:].astype(BF16)
    re = _dot(cl_ref[...], uc) - _dot(sl_ref[...], us)
    fg = fg_ref[...].astype(F32)
    o_ref[...] = (_dot(re.astype(BF16), wf_ref[...]) * (fg * jax.nn.sigmoid(fg))).astype(BF16)


def _fourier_ctx(p3, cs, cl, sl, wf):
    bsz, seq_len, _ = p3.shape
    gc = wf.shape[-1]
    return pl.pallas_call(
        _fourier_ctx_kernel,
        grid=(bsz, F_GROUPS),
        in_specs=[
            pl.BlockSpec((None, seq_len, gc), lambda b, g: (b, 0, g)),
            pl.BlockSpec((None, seq_len, gc), lambda b, g: (b, 0, F_GROUPS + g)),
            pl.BlockSpec((gc, 2 * gc), lambda b, g: (0, 0)),
            pl.BlockSpec((seq_len, seq_len), lambda b, g: (0, 0)),
            pl.BlockSpec((seq_len, seq_len), lambda b, g: (0, 0)),
            pl.BlockSpec((None, gc, gc), lambda b, g: (g, 0, 0)),
        ],
        out_specs=pl.BlockSpec((None, seq_len, gc), lambda b, g: (b, 0, g)),
        out_shape=jax.ShapeDtypeStruct((bsz, seq_len, F_GROUPS * gc), BF16),
        compiler_params=_params("parallel", "parallel"),
        name="fourier_ctx",
    )(p3, p3, cs, cl, sl, wf)


COL_TILE = 16


def _fourier_lat1_kernel(u_ref, rm_ref, y_ref):
    rows, cw, gc = u_ref.shape
    u = u_ref[...].reshape(rows * cw, gc)
    res = _dot(rm_ref[...], u).astype(BF16)
    y_ref[...] = res.reshape(2, rows, cw, gc)


def _fourier_lat1(p4, rm, gc):
    bsz, rows, cols, _ = p4.shape
    return pl.pallas_call(
        _fourier_lat1_kernel,
        grid=(bsz, cols // COL_TILE, F_GROUPS),
        in_specs=[
            pl.BlockSpec((None, rows, COL_TILE, gc), lambda b, cb, g: (b, 0, cb, g)),
            pl.BlockSpec((2 * rows * COL_TILE, rows * COL_TILE), lambda b, cb, g: (0, 0)),
        ],
        out_specs=pl.BlockSpec((None, None, 2, rows, COL_TILE, gc), lambda b, cb, g: (b, g, 0, 0, cb, 0)),
        out_shape=jax.ShapeDtypeStruct((bsz, F_GROUPS, 2, rows, cols, gc), BF16),
        compiler_params=_params("parallel", "parallel", "parallel"),
        name="fourier_lat1",
    )(p4, rm)


ROW_TILE = 16


def _fourier_lat2_kernel(y_ref, fg_ref, m2_ref, cs_ref, wf_ref, o_ref, zz_ref):
    _, rt, cols, gc = y_ref.shape
    sub = 2 * cols
    for s in range(rt // 2):
        rs = slice(s * sub, (s + 1) * sub)
        st = jnp.concatenate([y_ref[0, 2 * s:2 * s + 2].reshape(sub, gc),
                              y_ref[1, 2 * s:2 * s + 2].reshape(sub, gc)], axis=0)
        out = _dot(m2_ref[...], st)
        zz_ref[rs, :gc] = out[:sub].astype(BF16)
        zz_ref[rs, gc:] = out[sub:].astype(BF16)
    re = _dot(zz_ref[...], cs_ref[...])
    fg = fg_ref[...].astype(F32)
    o_ref[...] = (_dot(re.astype(BF16), wf_ref[...]) * (fg * jax.nn.sigmoid(fg))).astype(BF16)


def _fourier_lat2(y6, p2, m2, cs_stack, wf):
    bsz, _, _, rows, cols, gc = y6.shape
    tokens = ROW_TILE * cols
    tiles = rows // ROW_TILE
    return pl.pallas_call(
        _fourier_lat2_kernel,
        grid=(bsz, tiles, F_GROUPS),
        in_specs=[
            pl.BlockSpec((None, None, 2, ROW_TILE, cols, gc), lambda b, rb, g: (b, g, 0, rb, 0, 0)),
            pl.BlockSpec((tokens, gc), lambda b, rb, g: (b * tiles + rb, F_GROUPS + g)),
            pl.BlockSpec((4 * cols, 4 * cols), lambda b, rb, g: (0, 0)),
            pl.BlockSpec((2 * gc, gc), lambda b, rb, g: (0, 0)),
            pl.BlockSpec((None, gc, gc), lambda b, rb, g: (g, 0, 0)),
        ],
        out_specs=pl.BlockSpec((tokens, gc), lambda b, rb, g: (b * tiles + rb, g)),
        out_shape=jax.ShapeDtypeStruct((bsz * rows * cols, F_GROUPS * gc), BF16),
        scratch_shapes=[pltpu.VMEM((tokens, 2 * gc), BF16)],
        compiler_params=_params("parallel", "parallel", "parallel"),
        name="fourier_lat2",
    )(y6, p2, m2, cs_stack, wf)


OUT_COL_CHUNK = 1024


def _outproj_kernel(f_ref, g_ref, x_ref, mod_ref, fnw_ref, w_ref, o_ref):
    tm, fw = f_ref.shape
    d = w_ref.shape[-1]
    f = f_ref[...]
    g = g_ref[...]
    ss = jnp.zeros((tm, 1), F32)
    for c0 in range(0, d, OUT_COL_CHUNK):
        cs = slice(c0, c0 + OUT_COL_CHUNK)
        acc = _dot(f, w_ref[:fw, cs]) + _dot(g, w_ref[fw:, cs])
        pre = x_ref[:, cs] + mod_ref[2:3, cs] * acc
        ss = ss + jnp.sum(pre * pre, axis=-1, keepdims=True)
        o_ref[:, cs] = pre
    scale = lax.rsqrt(ss * (1.0 / d) + EPS)
    for c0 in range(0, d, OUT_COL_CHUNK):
        cs = slice(c0, c0 + OUT_COL_CHUNK)
        o_ref[:, cs] = (o_ref[:, cs] * scale) * fnw_ref[:, cs]


def _outproj(f2, g2, x2, mod3, fnw, w_bf, mod_row, name, tm=256):
    t, d = x2.shape
    fw = f2.shape[-1]
    return pl.pallas_call(
        _outproj_kernel,
        grid=(t // tm,),
        in_specs=[
            pl.BlockSpec((tm, fw), lambda i: (i, 0)),
            pl.BlockSpec((tm, d - fw), lambda i: (i, 0)),
            pl.BlockSpec((tm, d), lambda i: (i, 0)),
            pl.BlockSpec((None, 3, d), lambda i: (mod_row(i), 0, 0)),
            pl.BlockSpec((1, d), lambda i: (0, 0)),
            pl.BlockSpec((d, d), lambda i: (0, 0), pipeline_mode=pl.Buffered(1)),
        ],
        out_specs=pl.BlockSpec((tm, d), lambda i: (i, 0)),
        out_shape=jax.ShapeDtypeStruct((t, d), F32),
        compiler_params=_params("parallel"),
        name=name,
    )(f2, g2, x2, mod3, fnw, w_bf)


def kernel(x_prompt, x_sample, state_gla_fwd, state_gla_bwd, c, c_ctx, ada_w, ada_b, norm_w,
           w_in, w_alpha, b_alpha, w_fourier, gla_norm_w, w_out, final_norm_w):
    bp, lp, d = x_prompt.shape
    bs, ls, _ = x_sample.shape
    depth = ada_w.shape[0]
    assert depth == 1, "single-layer step"
    fw = d // 2
    gc = fw // F_GROUPS
    dk_total = w_alpha.shape[-1]
    n_main = 2 * fw + 2 * dk_total + 2 * (d - fw)
    gla_offsets = (2 * fw, 2 * fw + dk_total, 2 * fw + 2 * dk_total, 2 * fw + 2 * dk_total + (d - fw))
    rows = ls // GRID_W
    assert bs + 1 <= MOD_ROWS and rows == GRID_W and 8 * GLA_LOWRANK == LANES

    cc = jnp.zeros((MOD_ROWS, d), F32).at[:bs].set(c).at[bs].set(c_ctx)
    mod3 = _modulation(cc, ada_w[0], ada_b).reshape(MOD_ROWS, 3, d)

    w_all = w_in[0].astype(BF16)
    w_a = jnp.tile(w_in[0, :, n_main:], (1, LANES // (2 * GLA_LOWRANK))).astype(BF16)
    w_out_bf = w_out[0].astype(BF16)
    wf_bf = w_fourier[0].astype(BF16)
    nw = norm_w[0][None, :]
    fnw = final_norm_w[None, :]
    lr = GLA_LOWRANK
    wdir = jnp.zeros((2, 2 * lr, dk_total), F32).at[0, :lr].set(w_alpha[0, 0]).at[1, lr:].set(w_alpha[0, 1])
    wdir_hi = wdir.astype(BF16)
    wdir_lo = (wdir - wdir_hi.astype(F32)).astype(BF16)
    wz = jnp.concatenate([wdir_hi, wdir_hi, wdir_lo, jnp.zeros_like(wdir_hi)], axis=1)
    bal = b_alpha[0][:, None, :]
    gnw = gla_norm_w[0][:, None, :]

    cc_c, cc_s = _dft_mats(gc)
    cs = jnp.asarray(np.concatenate([cc_c, cc_s], axis=1)).astype(BF16)
    cs_stack = jnp.asarray(np.concatenate([cc_c, cc_s], axis=0)).astype(BF16)
    cl_c, cl_s = _dft_mats(lp)
    cl = jnp.asarray(cl_c).astype(BF16)
    sl = jnp.asarray(cl_s).astype(BF16)
    g_c, g_s = _dft_mats(GRID_W)
    eye2 = np.eye(2, dtype=np.float32)
    a2, b2 = np.kron(eye2, g_c), np.kron(eye2, g_s)
    m2 = jnp.asarray(np.block([[a2, b2], [-b2, a2]])).astype(BF16)
    eye_t = np.eye(COL_TILE, dtype=np.float32)
    rm = jnp.asarray(np.concatenate([np.kron(g_c, eye_t), -np.kron(g_s, eye_t)], axis=0)).astype(BF16)

    xp2 = x_prompt.reshape(bp * lp, d)
    ctx_row = lambda i: bs
    p_ctx, a_ctx = _inproj(xp2, mod3, nw, w_all, w_a, n_main, ctx_row, "inproj_ctx")
    p3 = p_ctx.reshape(bp, lp, n_main)
    g_ctx, new_f, new_b = _gla(p3, a_ctx.reshape(bp, lp, LANES), wz, bal, gnw,
                               None, None, True, gla_offsets, "gla_ctx")
    f_ctx = _fourier_ctx(p3, cs, cl, sl, wf_bf)
    y_prompt = _outproj(f_ctx.reshape(bp * lp, fw), g_ctx.reshape(bp * lp, d - fw),
                        xp2, mod3, fnw, w_out_bf, ctx_row, "outproj_ctx")

    xs2 = x_sample.reshape(bs * ls, d)
    tm_i = 1024
    p_lat, a_lat = _inproj(xs2, mod3, nw, w_all, w_a, n_main,
                           lambda i: i // (ls // tm_i), "inproj_lat", tm=tm_i)
    (g_lat,) = _gla(p_lat.reshape(bs, ls, n_main), a_lat.reshape(bs, ls, LANES), wz, bal, gnw,
                    state_gla_fwd, state_gla_bwd, False, gla_offsets, "gla_lat")
    y6 = _fourier_lat1(p_lat.reshape(bs, rows, GRID_W, n_main), rm, gc)
    f_lat = _fourier_lat2(y6, p_lat, m2, cs_stack, wf_bf)
    tm_o = 256
    y_sample = _outproj(f_lat, g_lat.reshape(bs * ls, d - fw), xs2, mod3, fnw,
                        w_out_bf, lambda i: i // (ls // tm_o), "outproj_lat", tm=tm_o)

    return (y_prompt.reshape(bp, lp, d), y_sample.reshape(bs, ls, d), new_f, new_b)
```

```python
import functools

import numpy as np
import jax
import jax.numpy as jnp
from jax import lax
from jax.experimental import pallas as pl
from jax.experimental.pallas import tpu as pltpu

F32 = jnp.float32
BF16 = jnp.bfloat16

GRID_W = 64
F_GROUPS = 4
GLA_HEADS = 4
GLA_LOWRANK = 16
GLA_TAU = 16.0
GLA_CHUNK = 64
GLA_BLOCK = 256
GLA_SLOTS = 4
EPS = 1e-6

VMEM_LIMIT_BYTES = 58 * 1024 * 1024
LANES = 128
MOD_ROWS = 16


def _params(*sem):
    return pltpu.CompilerParams(dimension_semantics=sem, vmem_limit_bytes=VMEM_LIMIT_BYTES)


def _dot(a, b):
    return jnp.dot(a, b, preferred_element_type=F32)


def _split2(x):
    hi = x.astype(BF16)
    lo = (x - hi.astype(F32)).astype(BF16)
    return hi, lo


def _dft_mats(n):
    k = np.arange(n)
    ang = 2.0 * np.pi * ((k[:, None] * k[None, :]) % n) / n
    s = 1.0 / np.sqrt(n)
    return (np.cos(ang) * s).astype(np.float32), (np.sin(ang) * s).astype(np.float32)


def _mod_kernel(c_ref, w_ref, b_ref, o_ref):
    c = c_ref[...]
    s = c * jax.nn.sigmoid(c)
    s_hi, s_lo = _split2(s)
    w_hi, w_lo = _split2(w_ref[...])
    o_ref[...] = _dot(s_hi, w_hi) + _dot(s_hi, w_lo) + _dot(s_lo, w_hi) + b_ref[...]


def _modulation(cc, ada_w, ada_b, tn=512):
    d, n = ada_w.shape
    return pl.pallas_call(
        _mod_kernel,
        grid=(n // tn,),
        in_specs=[
            pl.BlockSpec((MOD_ROWS, d), lambda j: (0, 0)),
            pl.BlockSpec((d, tn), lambda j: (0, j)),
            pl.BlockSpec((1, tn), lambda j: (0, j)),
        ],
        out_specs=pl.BlockSpec((MOD_ROWS, tn), lambda j: (0, j)),
        out_shape=jax.ShapeDtypeStruct((MOD_ROWS, n), F32),
        compiler_params=_params("parallel"),
        name="mod",
    )(cc, ada_w, ada_b)


def _inproj_kernel(x_ref, mod_ref, nw_ref, w_ref, wa_ref, p_ref, a_ref, h0_ref, h1_ref, *, n_slices):
    i = pl.program_id(0)
    j = pl.program_id(1)
    rows = x_ref.shape[0]
    odd = i % 2

    def step(h_cur, h_prev, first):
        if not first:
            p_ref[...] = _dot(h_prev[...], w_ref[...]).astype(BF16)
        x = x_ref[...]
        ms = jnp.mean(x * x, axis=-1, keepdims=True)
        y = (x * lax.rsqrt(ms + EPS)) * nw_ref[...]
        h = y * (1.0 + mod_ref[1:2, :]) + mod_ref[0:1, :]
        start = pl.multiple_of(jnp.minimum(j, n_slices - 1) * rows, rows)
        h_cur[pl.ds(start, rows), :] = h.astype(BF16)
        if first:
            return

        @pl.when(j == 0)
        def _():
            a_ref[...] = _dot(h_prev[...], wa_ref[...])

    @pl.when(i == 0)
    def _():
        step(h0_ref, h1_ref, True)

    @pl.when((i > 0) & (odd == 0))
    def _():
        step(h0_ref, h1_ref, False)

    @pl.when(odd == 1)
    def _():
        step(h1_ref, h0_ref, False)


def _inproj(x2, mod3, nw, w_bf, wa_bf, n_main, mod_row, name, tm=1024, tn=512, n_slices=16):
    t, d = x2.shape
    n_i, n_j = t // tm, n_main // tn
    rows = tm // n_slices
    assert n_slices <= n_j and rows % 8 == 0
    tile = lambda i: jnp.minimum(i, n_i - 1)
    done = lambda i: jnp.maximum(i - 1, 0)
    col = lambda i, j: jnp.where(i == 0, 0, j)
    return pl.pallas_call(
        functools.partial(_inproj_kernel, n_slices=n_slices),
        grid=(n_i + 1, n_j),
        in_specs=[
            pl.BlockSpec((rows, d), lambda i, j: (tile(i) * n_slices + jnp.minimum(j, n_slices - 1), 0)),
            pl.BlockSpec((None, 3, d), lambda i, j: (mod_row(tile(i)), 0, 0)),
            pl.BlockSpec((1, d), lambda i, j: (0, 0)),
            pl.BlockSpec((d, tn), lambda i, j: (0, col(i, j))),
            pl.BlockSpec((d, LANES), lambda i, j: (0, 0)),
        ],
        out_specs=[
            pl.BlockSpec((tm, tn), lambda i, j: (done(i), col(i, j))),
            pl.BlockSpec((tm, LANES), lambda i, j: (done(i), 0)),
        ],
        out_shape=[
            jax.ShapeDtypeStruct((t, n_main), BF16),
            jax.ShapeDtypeStruct((t, LANES), F32),
        ],
        scratch_shapes=[pltpu.VMEM((tm, d), BF16), pltpu.VMEM((tm, d), BF16)],
        compiler_params=_params("arbitrary", "arbitrary"),
        name=name,
    )(x2, mod3, nw, w_bf, wa_bf)


def _gla_kernel(*refs, seq_len, has_state, emit_state):
    q_ref, k_ref, v_ref, gg_ref, a_ref, wz_ref, bal_ref, gnw_ref, tri_ref = refs[:9]
    pos = 9
    if has_state:
        s0f_ref, s0b_ref = refs[pos:pos + 2]
        pos += 2
    o_ref = refs[pos]
    pos += 1
    if emit_state:
        sf_ref, sb_ref = refs[pos:pos + 2]
        pos += 2
    of_ref, s_ref, b_scr, dcol_scr, qh_scr, kh_scr = refs[pos:pos + 6]

    c = GLA_CHUNK
    n_sub = GLA_BLOCK // c
    n_blocks = seq_len // GLA_BLOCK
    dk = q_ref.shape[-1]
    q_scale = float(dk) ** -0.5
    lrow = lax.broadcasted_iota(jnp.int32, (c, GLA_BLOCK), 0)
    bcol = lax.broadcasted_iota(jnp.int32, (c, GLA_BLOCK), 1)
    ccol = bcol // c
    lcol = bcol - ccol * c
    a_group = lax.broadcasted_iota(jnp.int32, (GLA_BLOCK, LANES), 1) // (2 * GLA_LOWRANK)
    contract_last = (((1,), (1,)), ((), ()))
    contract_first = (((0,), (0,)), ((), ()))
    sub = lambda x, j: x[j * c:(j + 1) * c]

    def rows_of(blk):
        start = blk * GLA_BLOCK
        if not isinstance(blk, int):
            start = pl.multiple_of(start, GLA_BLOCK)
        return pl.ds(start, GLA_BLOCK)

    def slot_of(blk, direction):
        return direction * GLA_SLOTS + blk % GLA_SLOTS

    def chunk_masks(fwd):
        return [(ccol == j) & ((lrow >= lcol) if fwd else (lcol >= lrow)) for j in range(n_sub)]

    def stage_decay(blk, direction):
        a = a_ref[rows_of(blk), :]
        hi = a.astype(BF16).astype(F32)
        lhs = jnp.where(a_group == 1, a - hi, jnp.where(a_group == 3, 0.0, hi)).astype(BF16)
        z = _dot(lhs, wz_ref[direction]) + bal_ref[direction]
        g = (jnp.minimum(z, 0.0) - jnp.log(1.0 + jnp.exp(-jnp.abs(z)))) * (1.0 / GLA_TAU)
        tri = tri_ref[direction]
        g_hi, g_lo = _split2(g)
        b_scr[slot_of(blk, direction)] = _dot(tri, g_hi) + _dot(tri, g_lo)
        dcol_scr[slot_of(blk, direction)] = jnp.exp(jnp.sum(g.T, axis=-1, keepdims=True))

    def stage_intra(blk, direction, first):
        fwd = direction == 0
        rs = rows_of(blk)
        scan = list(range(n_sub)) if fwd else list(range(n_sub - 1, -1, -1))
        at = {j: p for p, j in enumerate(scan)}
        masks0 = chunk_masks(fwd)
        b = b_scr[slot_of(blk, direction)]
        t = [sub(b, j)[c - 1:c, :] if fwd else sub(b, j)[0:1, :] for j in range(n_sub)]
        b_end = jnp.concatenate([jnp.broadcast_to(t[j], (c, dk)) for j in range(n_sub)], axis=0)
        q = q_ref[rs, :].astype(F32)
        k = k_ref[rs, :].astype(F32)
        q32 = q * q_scale * jnp.exp(b)
        q_t = q32.astype(BF16)
        k_t = (k * jnp.exp(-b)).astype(BF16)
        kd32 = k * jnp.exp(b_end - b)
        k_dec = kd32.astype(BF16)

        before, acc = {}, jnp.zeros_like(t[0])
        for j in scan:
            before[j] = acc
            acc = acc + t[j]
        total = acc
        qh_scr[slot_of(blk, direction)] = jnp.concatenate(
            [sub(q32, j) * jnp.exp(before[j]) for j in range(n_sub)], axis=0).astype(BF16)
        kh_scr[slot_of(blk, direction)] = jnp.concatenate(
            [sub(kd32, j) * jnp.exp(total - before[j] - t[j]) for j in range(n_sub)], axis=0).astype(BF16)

        lhs, where = [q_t], {}
        n_rows = GLA_BLOCK
        for d in range(2, n_sub):
            for j in scan[d:]:
                between = scan[at[j] - d + 1:at[j]]
                gap = t[between[0]]
                for i in between[1:]:
                    gap = gap + t[i]
                lhs.append((sub(q32, j) * jnp.exp(gap)).astype(BF16))
                where[(j, d)] = n_rows
                n_rows += c
        r = lax.dot_general(jnp.concatenate(lhs, axis=0), k_dec, contract_last, preferred_element_type=F32)
        a0 = lax.dot_general(q_t, k_t, contract_last, preferred_element_type=F32)
        att_rows = []
        for j in range(n_sub):
            att = jnp.where(masks0[j], sub(a0, j), 0.0)
            for d in range(1, at[j] + 1):
                src = sub(r, j) if d == 1 else r[where[(j, d)]:where[(j, d)] + c]
                att = jnp.where(ccol == scan[at[j] - d], src, att)
            att_rows.append(att)
        att = jnp.concatenate(att_rows, axis=0).astype(BF16)
        o_intra = _dot(att, v_ref[rs, :])
        if first:
            of_ref[rs, :] = o_intra
        else:
            of_ref[rs, :] += o_intra

    def stage_state(blk, direction, first):
        rs = rows_of(blk)
        s = s_ref[direction]
        o_inter = _dot(qh_scr[slot_of(blk, direction)], s.astype(BF16))
        upd = lax.dot_general(kh_scr[slot_of(blk, direction)], v_ref[rs, :], contract_first, preferred_element_type=F32)
        s_ref[direction] = dcol_scr[slot_of(blk, direction)] * s + upd
        if first:
            of_ref[rs, :] += o_inter
        else:
            o = of_ref[rs, :] + o_inter
            ms = jnp.mean(o * o, axis=-1, keepdims=True)
            y = (o * lax.rsqrt(ms + EPS)) * gnw_ref[...]
            gg = gg_ref[rs, :].astype(F32)
            o_ref[rs, :] = (y * (gg * jax.nn.sigmoid(gg))).astype(BF16)

    n = n_blocks
    if has_state:
        s_ref[0] = s0f_ref[...]
        s_ref[1] = s0b_ref[...]
    else:
        s_ref[...] = jnp.zeros_like(s_ref)

    def both(stage, i, *args):
        stage(i, 0, *args)
        stage(n - 1 - i, 1, *args)

    if n == 1:
        both(stage_decay, 0)
        stage_intra(0, 0, True)
        stage_intra(0, 1, False)
        stage_state(0, 0, True)
        stage_state(0, 1, False)
    else:
        half = n // 2
        both(stage_decay, 0)
        both(stage_decay, 1)
        both(stage_intra, 0, True)

        def first_half(i, carry):
            both(stage_state, i, True)
            both(stage_intra, i + 1, True)
            both(stage_decay, i + 2)
            return carry

        lax.fori_loop(0, half - 1, first_half, 0)
        both(stage_state, half - 1, True)
        both(stage_intra, half, False)
        both(stage_decay, half + 1)

        def second_half(i, carry):
            both(stage_state, i, False)
            both(stage_intra, i + 1, False)
            both(stage_decay, i + 2)
            return carry

        lax.fori_loop(half, n - 2, second_half, 0)
        both(stage_state, n - 2, False)
        both(stage_intra, n - 1, False)
        both(stage_state, n - 1, False)
    if emit_state:
        sf_ref[...] = s_ref[0]
        sb_ref[...] = s_ref[1]


def _gla(p3, a3, wz, bal, gnw, tri, s0f, s0b, emit_state, offsets, name):
    bsz, seq_len, _ = p3.shape
    dk = wz.shape[-1] // GLA_HEADS
    dv = gnw.shape[-1]
    off_q, off_k, off_v, off_gg = offsets
    assert off_q % dk == 0 and off_k % dk == 0 and off_v % dv == 0 and off_gg % dv == 0
    n_blocks = seq_len // GLA_BLOCK
    assert seq_len % GLA_BLOCK == 0 and GLA_BLOCK % GLA_CHUNK == 0
    assert n_blocks == 1 or (n_blocks % 2 == 0 and n_blocks >= 4)
    qb, kb, vb, ggb = off_q // dk, off_k // dk, off_v // dv, off_gg // dv
    has_state = s0f is not None
    in_specs = [
        pl.BlockSpec((None, seq_len, dk), lambda b, h: (b, 0, qb + h)),
        pl.BlockSpec((None, seq_len, dk), lambda b, h: (b, 0, kb + h)),
        pl.BlockSpec((None, seq_len, dv), lambda b, h: (b, 0, vb + h)),
        pl.BlockSpec((None, seq_len, dv), lambda b, h: (b, 0, ggb + h)),
        pl.BlockSpec((None, seq_len, LANES), lambda b, h: (b, 0, 0)),
        pl.BlockSpec((2, LANES, dk), lambda b, h: (0, 0, h)),
        pl.BlockSpec((2, 1, dk), lambda b, h: (0, 0, h)),
        pl.BlockSpec((None, 1, dv), lambda b, h: (h, 0, 0)),
        pl.BlockSpec((2, GLA_BLOCK, GLA_BLOCK), lambda b, h: (0, 0, 0)),
    ]
    args = [p3, p3, p3, p3, a3, wz, bal, gnw, tri]
    state_spec = pl.BlockSpec((None, None, None, dk, dv), lambda b, h: (b, 0, h, 0, 0))
    if has_state:
        in_specs += [state_spec, state_spec]
        args += [s0f, s0b]
    out_specs = [pl.BlockSpec((None, seq_len, dv), lambda b, h: (b, 0, h))]
    out_shape = [jax.ShapeDtypeStruct((bsz, seq_len, GLA_HEADS * dv), BF16)]
    if emit_state:
        out_specs += [state_spec, state_spec]
        out_shape += [jax.ShapeDtypeStruct((bsz, 1, GLA_HEADS, dk, dv), F32)] * 2
    return pl.pallas_call(
        functools.partial(_gla_kernel, seq_len=seq_len, has_state=has_state, emit_state=emit_state),
        grid=(bsz, GLA_HEADS),
        in_specs=in_specs,
        out_specs=out_specs,
        out_shape=out_shape,
        scratch_shapes=[
            pltpu.VMEM((seq_len, dv), F32),
            pltpu.VMEM((2, dk, dv), F32),
            pltpu.VMEM((2 * GLA_SLOTS, GLA_BLOCK, dk), F32),
            pltpu.VMEM((2 * GLA_SLOTS, dk, 1), F32),
            pltpu.VMEM((2 * GLA_SLOTS, GLA_BLOCK, dk), BF16),
            pltpu.VMEM((2 * GLA_SLOTS, GLA_BLOCK, dk), BF16),
        ],
        compiler_params=_params("parallel", "parallel"),
        name=name,
    )(*args)


def _fourier_ctx_kernel(u_ref, fg_ref, cs_ref, cl_ref, sl_ref, wf_ref, o_ref, re_ref):
    nb, seq_len, gc = u_ref.shape
    res = _dot(u_ref[...].reshape(nb * seq_len, gc), cs_ref[...])
    for i in range(nb):
        rs = slice(i * seq_len, (i + 1) * seq_len)
        uc = res[rs, :gc].astype(BF16)
        us = res[rs, gc:].astype(BF16)
        re_ref[rs, :] = (_dot(cl_ref[...], uc) - _dot(sl_ref[...], us)).astype(BF16)
    fg = fg_ref[...].reshape(nb * seq_len, gc).astype(F32)
    out = _dot(re_ref[...], wf_ref[...]) * (fg * jax.nn.sigmoid(fg))
    o_ref[...] = out.astype(BF16).reshape(nb, seq_len, gc)


def _fourier_ctx(p3, cs, cl, sl, wf, nb=4):
    bsz, seq_len, _ = p3.shape
    gc = wf.shape[-1]
    return pl.pallas_call(
        _fourier_ctx_kernel,
        grid=(bsz // nb, F_GROUPS),
        in_specs=[
            pl.BlockSpec((nb, seq_len, gc), lambda b, g: (b, 0, g)),
            pl.BlockSpec((nb, seq_len, gc), lambda b, g: (b, 0, F_GROUPS + g)),
            pl.BlockSpec((gc, 2 * gc), lambda b, g: (0, 0)),
            pl.BlockSpec((seq_len, seq_len), lambda b, g: (0, 0)),
            pl.BlockSpec((seq_len, seq_len), lambda b, g: (0, 0)),
            pl.BlockSpec((None, gc, gc), lambda b, g: (g, 0, 0)),
        ],
        out_specs=pl.BlockSpec((nb, seq_len, gc), lambda b, g: (b, 0, g)),
        out_shape=jax.ShapeDtypeStruct((bsz, seq_len, F_GROUPS * gc), BF16),
        scratch_shapes=[pltpu.VMEM((nb * seq_len, gc), BF16)],
        compiler_params=_params("parallel", "parallel"),
        name="fourier_ctx",
    )(p3, p3, cs, cl, sl, wf)


COL_TILE = 16


def _fourier_lat1_kernel(u_ref, rm_ref, y_ref):
    rows, cw, gc = u_ref.shape
    u = u_ref[...].reshape(rows * cw, gc)
    res = _dot(rm_ref[...], u).astype(BF16)
    y_ref[...] = res.reshape(2, rows, cw, gc)


def _fourier_lat1(p4, rm, gc):
    bsz, rows, cols, _ = p4.shape
    return pl.pallas_call(
        _fourier_lat1_kernel,
        grid=(bsz, cols // COL_TILE, F_GROUPS),
        in_specs=[
            pl.BlockSpec((None, rows, COL_TILE, gc), lambda b, cb, g: (b, 0, cb, g)),
            pl.BlockSpec((2 * rows * COL_TILE, rows * COL_TILE), lambda b, cb, g: (0, 0)),
        ],
        out_specs=pl.BlockSpec((None, None, 2, rows, COL_TILE, gc), lambda b, cb, g: (b, g, 0, 0, cb, 0)),
        out_shape=jax.ShapeDtypeStruct((bsz, F_GROUPS, 2, rows, cols, gc), BF16),
        compiler_params=_params("parallel", "parallel", "parallel"),
        name="fourier_lat1",
    )(p4, rm)


ROW_TILE = 16


def _fourier_lat2_kernel(y_ref, fg_ref, m2_ref, cs_ref, wf_ref, o_ref, zz_ref):
    _, rt, cols, gc = y_ref.shape
    sub = 2 * cols
    for s in range(rt // 2):
        rs = slice(s * sub, (s + 1) * sub)
        st = jnp.concatenate([y_ref[0, 2 * s:2 * s + 2].reshape(sub, gc),
                              y_ref[1, 2 * s:2 * s + 2].reshape(sub, gc)], axis=0)
        out = _dot(m2_ref[...], st)
        zz_ref[rs, :gc] = out[:sub].astype(BF16)
        zz_ref[rs, gc:] = out[sub:].astype(BF16)
    re = _dot(zz_ref[...], cs_ref[...])
    fg = fg_ref[...].astype(F32)
    o_ref[...] = (_dot(re.astype(BF16), wf_ref[...]) * (fg * jax.nn.sigmoid(fg))).astype(BF16)


def _fourier_lat2(y6, p2, m2, cs_stack, wf):
    bsz, _, _, rows, cols, gc = y6.shape
    tokens = ROW_TILE * cols
    tiles = rows // ROW_TILE
    return pl.pallas_call(
        _fourier_lat2_kernel,
        grid=(bsz, tiles, F_GROUPS),
        in_specs=[
            pl.BlockSpec((None, None, 2, ROW_TILE, cols, gc), lambda b, rb, g: (b, g, 0, rb, 0, 0)),
            pl.BlockSpec((tokens, gc), lambda b, rb, g: (b * tiles + rb, F_GROUPS + g)),
            pl.BlockSpec((4 * cols, 4 * cols), lambda b, rb, g: (0, 0)),
            pl.BlockSpec((2 * gc, gc), lambda b, rb, g: (0, 0)),
            pl.BlockSpec((None, gc, gc), lambda b, rb, g: (g, 0, 0)),
        ],
        out_specs=pl.BlockSpec((tokens, gc), lambda b, rb, g: (b * tiles + rb, g)),
        out_shape=jax.ShapeDtypeStruct((bsz * rows * cols, F_GROUPS * gc), BF16),
        scratch_shapes=[pltpu.VMEM((tokens, 2 * gc), BF16)],
        compiler_params=_params("parallel", "parallel", "parallel"),
        name="fourier_lat2",
    )(y6, p2, m2, cs_stack, wf)


OUT_COL_CHUNK = 1024


def _outproj_kernel(f_ref, g_ref, x_ref, mod_ref, fnw_ref, w_ref, o_ref):
    tm, fw = f_ref.shape
    d = w_ref.shape[-1]
    f = f_ref[...]
    g = g_ref[...]
    ss = jnp.zeros((tm, 1), F32)
    for c0 in range(0, d, OUT_COL_CHUNK):
        cs = slice(c0, c0 + OUT_COL_CHUNK)
        acc = _dot(f, w_ref[:fw, cs]) + _dot(g, w_ref[fw:, cs])
        pre = x_ref[:, cs] + mod_ref[2:3, cs] * acc
        ss = ss + jnp.sum(pre * pre, axis=-1, keepdims=True)
        o_ref[:, cs] = pre
    scale = lax.rsqrt(ss * (1.0 / d) + EPS)
    for c0 in range(0, d, OUT_COL_CHUNK):
        cs = slice(c0, c0 + OUT_COL_CHUNK)
        o_ref[:, cs] = (o_ref[:, cs] * scale) * fnw_ref[:, cs]


def _outproj(f2, g2, x2, mod3, fnw, w_bf, mod_row, name, tm=256):
    t, d = x2.shape
    fw = f2.shape[-1]
    return pl.pallas_call(
        _outproj_kernel,
        grid=(t // tm,),
        in_specs=[
            pl.BlockSpec((tm, fw), lambda i: (i, 0)),
            pl.BlockSpec((tm, d - fw), lambda i: (i, 0)),
            pl.BlockSpec((tm, d), lambda i: (i, 0)),
            pl.BlockSpec((None, 3, d), lambda i: (mod_row(i), 0, 0)),
            pl.BlockSpec((1, d), lambda i: (0, 0)),
            pl.BlockSpec((d, d), lambda i: (0, 0), pipeline_mode=pl.Buffered(1)),
        ],
        out_specs=pl.BlockSpec((tm, d), lambda i: (i, 0)),
        out_shape=jax.ShapeDtypeStruct((t, d), F32),
        compiler_params=_params("parallel"),
        name=name,
    )(f2, g2, x2, mod3, fnw, w_bf)


def kernel(x_prompt, x_sample, state_gla_fwd, state_gla_bwd, c, c_ctx, ada_w, ada_b, norm_w,
           w_in, w_alpha, b_alpha, w_fourier, gla_norm_w, w_out, final_norm_w):
    bp, lp, d = x_prompt.shape
    bs, ls, _ = x_sample.shape
    depth = ada_w.shape[0]
    assert depth == 1, "single-layer step"
    fw = d // 2
    gc = fw // F_GROUPS
    dk_total = w_alpha.shape[-1]
    n_main = 2 * fw + 2 * dk_total + 2 * (d - fw)
    gla_offsets = (2 * fw, 2 * fw + dk_total, 2 * fw + 2 * dk_total, 2 * fw + 2 * dk_total + (d - fw))
    rows = ls // GRID_W
    assert bs + 1 <= MOD_ROWS and rows == GRID_W and 8 * GLA_LOWRANK == LANES

    cc = jnp.zeros((MOD_ROWS, d), F32).at[:bs].set(c).at[bs].set(c_ctx)
    mod3 = _modulation(cc, ada_w[0], ada_b).reshape(MOD_ROWS, 3, d)

    w_all = w_in[0].astype(BF16)
    w_a = jnp.tile(w_in[0, :, n_main:], (1, LANES // (2 * GLA_LOWRANK))).astype(BF16)
    w_out_bf = w_out[0].astype(BF16)
    wf_bf = w_fourier[0].astype(BF16)
    nw = norm_w[0][None, :]
    fnw = final_norm_w[None, :]
    lr = GLA_LOWRANK
    wdir = jnp.zeros((2, 2 * lr, dk_total), F32).at[0, :lr].set(w_alpha[0, 0]).at[1, lr:].set(w_alpha[0, 1])
    wdir_hi = wdir.astype(BF16)
    wdir_lo = (wdir - wdir_hi.astype(F32)).astype(BF16)
    wz = jnp.concatenate([wdir_hi, wdir_hi, wdir_lo, jnp.zeros_like(wdir_hi)], axis=1)
    bal = b_alpha[0][:, None, :]
    gnw = gla_norm_w[0][:, None, :]
    pos_in_block = np.arange(GLA_BLOCK)
    same_chunk = (pos_in_block[:, None] // GLA_CHUNK) == (pos_in_block[None, :] // GLA_CHUNK)
    lower = pos_in_block[:, None] >= pos_in_block[None, :]
    tri = jnp.asarray(np.stack([same_chunk & lower, same_chunk & lower.T]).astype(np.float32)).astype(BF16)

    cc_c, cc_s = _dft_mats(gc)
    cs = jnp.asarray(np.concatenate([cc_c, cc_s], axis=1)).astype(BF16)
    cs_stack = jnp.asarray(np.concatenate([cc_c, cc_s], axis=0)).astype(BF16)
    cl_c, cl_s = _dft_mats(lp)
    cl = jnp.asarray(cl_c).astype(BF16)
    sl = jnp.asarray(cl_s).astype(BF16)
    g_c, g_s = _dft_mats(GRID_W)
    eye2 = np.eye(2, dtype=np.float32)
    a2, b2 = np.kron(eye2, g_c), np.kron(eye2, g_s)
    m2 = jnp.asarray(np.block([[a2, b2], [-b2, a2]])).astype(BF16)
    eye_t = np.eye(COL_TILE, dtype=np.float32)
    rm = jnp.asarray(np.concatenate([np.kron(g_c, eye_t), -np.kron(g_s, eye_t)], axis=0)).astype(BF16)

    xp2 = x_prompt.reshape(bp * lp, d)
    ctx_row = lambda i: bs
    p_ctx, a_ctx = _inproj(xp2, mod3, nw, w_all, w_a, n_main, ctx_row, "inproj_ctx")
    p3 = p_ctx.reshape(bp, lp, n_main)
    g_ctx, new_f, new_b = _gla(p3, a_ctx.reshape(bp, lp, LANES), wz, bal, gnw, tri,
                               None, None, True, gla_offsets, "gla_ctx")
    f_ctx = _fourier_ctx(p3, cs, cl, sl, wf_bf)
    y_prompt = _outproj(f_ctx.reshape(bp * lp, fw), g_ctx.reshape(bp * lp, d - fw),
                        xp2, mod3, fnw, w_out_bf, ctx_row, "outproj_ctx")

    xs2 = x_sample.reshape(bs * ls, d)
    tm_i = 1024
    p_lat, a_lat = _inproj(xs2, mod3, nw, w_all, w_a, n_main,
                           lambda i: i // (ls // tm_i), "inproj_lat", tm=tm_i)
    (g_lat,) = _gla(p_lat.reshape(bs, ls, n_main), a_lat.reshape(bs, ls, LANES), wz, bal, gnw, tri,
                    state_gla_fwd, state_gla_bwd, False, gla_offsets, "gla_lat")
    y6 = _fourier_lat1(p_lat.reshape(bs, rows, GRID_W, n_main), rm, gc)
    f_lat = _fourier_lat2(y6, p_lat, m2, cs_stack, wf_bf)
    tm_o = 256
    y_sample = _outproj(f_lat, g_lat.reshape(bs * ls, d - fw), xs2, mod3, fnw,
                        w_out_bf, lambda i: i // (ls // tm_o), "outproj_lat", tm=tm_o)

    return (y_prompt.reshape(bp, lp, d), y_sample.reshape(bs, ls, d), new_f, new_b)
```

```python
import functools

import numpy as np
import jax
import jax.numpy as jnp
from jax import lax
from jax.experimental import pallas as pl
from jax.experimental.pallas import tpu as pltpu

F32 = jnp.float32
BF16 = jnp.bfloat16

GRID_W = 64
F_GROUPS = 4
GLA_HEADS = 4
GLA_LOWRANK = 16
GLA_TAU = 16.0
GLA_CHUNK = 64
GLA_BLOCK = 256
GLA_SLOTS = 4
EPS = 1e-6

VMEM_LIMIT_BYTES = 58 * 1024 * 1024
LANES = 128
MOD_ROWS = 16


def _params(*sem):
    return pltpu.CompilerParams(dimension_semantics=sem, vmem_limit_bytes=VMEM_LIMIT_BYTES)


def _dot(a, b):
    return jnp.dot(a, b, preferred_element_type=F32)


def _split2(x):
    hi = x.astype(BF16)
    lo = (x - hi.astype(F32)).astype(BF16)
    return hi, lo


def _dft_mats(n):
    k = np.arange(n)
    ang = 2.0 * np.pi * ((k[:, None] * k[None, :]) % n) / n
    s = 1.0 / np.sqrt(n)
    return (np.cos(ang) * s).astype(np.float32), (np.sin(ang) * s).astype(np.float32)


def _mod_kernel(c_ref, w_ref, b_ref, o_ref):
    c = c_ref[...]
    s = c * jax.nn.sigmoid(c)
    s_hi, s_lo = _split2(s)
    w_hi, w_lo = _split2(w_ref[...])
    o_ref[...] = _dot(s_hi, w_hi) + _dot(s_hi, w_lo) + _dot(s_lo, w_hi) + b_ref[...]


def _modulation(cc, ada_w, ada_b, tn=512):
    d, n = ada_w.shape
    return pl.pallas_call(
        _mod_kernel,
        grid=(n // tn,),
        in_specs=[
            pl.BlockSpec((MOD_ROWS, d), lambda j: (0, 0)),
            pl.BlockSpec((d, tn), lambda j: (0, j)),
            pl.BlockSpec((1, tn), lambda j: (0, j)),
        ],
        out_specs=pl.BlockSpec((MOD_ROWS, tn), lambda j: (0, j)),
        out_shape=jax.ShapeDtypeStruct((MOD_ROWS, n), F32),
        compiler_params=_params("parallel"),
        name="mod",
    )(cc, ada_w, ada_b)


def _inproj_kernel(x_ref, mod_ref, nw_ref, w_ref, wa_ref, p_ref, a_ref, h0_ref, h1_ref, *, n_slices):
    i = pl.program_id(0)
    j = pl.program_id(1)
    rows = x_ref.shape[0]
    odd = i % 2

    def step(h_cur, h_prev, first):
        if not first:
            p_ref[...] = _dot(h_prev[...], w_ref[...]).astype(BF16)
        x = x_ref[...]
        ms = jnp.mean(x * x, axis=-1, keepdims=True)
        y = (x * lax.rsqrt(ms + EPS)) * nw_ref[...]
        h = y * (1.0 + mod_ref[1:2, :]) + mod_ref[0:1, :]
        start = pl.multiple_of(jnp.minimum(j, n_slices - 1) * rows, rows)
        h_cur[pl.ds(start, rows), :] = h.astype(BF16)
        if first:
            return

        @pl.when(j == 0)
        def _():
            a_ref[...] = _dot(h_prev[...], wa_ref[...])

    @pl.when(i == 0)
    def _():
        step(h0_ref, h1_ref, True)

    @pl.when((i > 0) & (odd == 0))
    def _():
        step(h0_ref, h1_ref, False)

    @pl.when(odd == 1)
    def _():
        step(h1_ref, h0_ref, False)


def _inproj(x2, mod3, nw, w_bf, wa_bf, n_main, mod_row, name, tm=1024, tn=512, n_slices=16):
    t, d = x2.shape
    n_i, n_j = t // tm, n_main // tn
    rows = tm // n_slices
    assert n_slices <= n_j and rows % 8 == 0
    tile = lambda i: jnp.minimum(i, n_i - 1)
    done = lambda i: jnp.maximum(i - 1, 0)
    col = lambda i, j: jnp.where(i == 0, 0, j)
    return pl.pallas_call(
        functools.partial(_inproj_kernel, n_slices=n_slices),
        grid=(n_i + 1, n_j),
        in_specs=[
            pl.BlockSpec((rows, d), lambda i, j: (tile(i) * n_slices + jnp.minimum(j, n_slices - 1), 0)),
            pl.BlockSpec((None, 3, d), lambda i, j: (mod_row(tile(i)), 0, 0)),
            pl.BlockSpec((1, d), lambda i, j: (0, 0)),
            pl.BlockSpec((d, tn), lambda i, j: (0, col(i, j))),
            pl.BlockSpec((d, LANES), lambda i, j: (0, 0)),
        ],
        out_specs=[
            pl.BlockSpec((tm, tn), lambda i, j: (done(i), col(i, j))),
            pl.BlockSpec((tm, LANES), lambda i, j: (done(i), 0)),
        ],
        out_shape=[
            jax.ShapeDtypeStruct((t, n_main), BF16),
            jax.ShapeDtypeStruct((t, LANES), F32),
        ],
        scratch_shapes=[pltpu.VMEM((tm, d), BF16), pltpu.VMEM((tm, d), BF16)],
        compiler_params=_params("arbitrary", "arbitrary"),
        name=name,
    )(x2, mod3, nw, w_bf, wa_bf)


def _gla_kernel(*refs, seq_len, has_state, emit_state):
    q_ref, k_ref, v_ref, gg_ref, a_ref, wz_ref, bal_ref, gnw_ref, tri_ref = refs[:9]
    pos = 9
    if has_state:
        s0f_ref, s0b_ref = refs[pos:pos + 2]
        pos += 2
    o_ref = refs[pos]
    pos += 1
    if emit_state:
        sf_ref, sb_ref = refs[pos:pos + 2]
        pos += 2
    of_ref, s_ref, b_scr, dcol_scr, qh_scr, kh_scr = refs[pos:pos + 6]

    c = GLA_CHUNK
    n_sub = GLA_BLOCK // c
    n_blocks = seq_len // GLA_BLOCK
    dk = q_ref.shape[-1]
    q_scale = float(dk) ** -0.5
    lrow = lax.broadcasted_iota(jnp.int32, (c, GLA_BLOCK), 0)
    bcol = lax.broadcasted_iota(jnp.int32, (c, GLA_BLOCK), 1)
    ccol = bcol // c
    lcol = bcol - ccol * c
    a_group = lax.broadcasted_iota(jnp.int32, (GLA_BLOCK, LANES), 1) // (2 * GLA_LOWRANK)
    contract_last = (((1,), (1,)), ((), ()))
    contract_first = (((0,), (0,)), ((), ()))
    sub = lambda x, j: x[j * c:(j + 1) * c]

    def rows_of(blk):
        start = blk * GLA_BLOCK
        if not isinstance(blk, int):
            start = pl.multiple_of(start, GLA_BLOCK)
        return pl.ds(start, GLA_BLOCK)

    def slot_of(blk, direction):
        return direction * GLA_SLOTS + blk % GLA_SLOTS

    def chunk_masks(fwd):
        return [(ccol == j) & ((lrow >= lcol) if fwd else (lcol >= lrow)) for j in range(n_sub)]

    def stage_decay(blk, direction):
        a = a_ref[rows_of(blk), :]
        hi = a.astype(BF16).astype(F32)
        lhs = jnp.where(a_group == 1, a - hi, jnp.where(a_group == 3, 0.0, hi)).astype(BF16)
        z = _dot(lhs, wz_ref[direction]) + bal_ref[direction]
        g = (jnp.minimum(z, 0.0) - jnp.log(1.0 + jnp.exp(-jnp.abs(z)))) * (1.0 / GLA_TAU)
        tri = tri_ref[direction]
        g_hi, g_lo = _split2(g)
        b_scr[slot_of(blk, direction)] = _dot(tri, g_hi) + _dot(tri, g_lo)
        dcol_scr[slot_of(blk, direction)] = jnp.exp(jnp.sum(g.T, axis=-1, keepdims=True))

    def stage_intra(blk, direction, first):
        fwd = direction == 0
        rs = rows_of(blk)
        scan = list(range(n_sub)) if fwd else list(range(n_sub - 1, -1, -1))
        at = {j: p for p, j in enumerate(scan)}
        masks0 = chunk_masks(fwd)
        b = b_scr[slot_of(blk, direction)]
        t = [sub(b, j)[c - 1:c, :] if fwd else sub(b, j)[0:1, :] for j in range(n_sub)]
        b_end = jnp.concatenate([jnp.broadcast_to(t[j], (c, dk)) for j in range(n_sub)], axis=0)
        q = q_ref[rs, :].astype(F32)
        k = k_ref[rs, :].astype(F32)
        q32 = q * q_scale * jnp.exp(b)
        q_t = q32.astype(BF16)
        k_t = (k * jnp.exp(-b)).astype(BF16)
        kd32 = k * jnp.exp(b_end - b)
        k_dec = kd32.astype(BF16)

        before, acc = {}, jnp.zeros_like(t[0])
        for j in scan:
            before[j] = acc
            acc = acc + t[j]
        total = acc
        qh_scr[slot_of(blk, direction)] = jnp.concatenate(
            [sub(q32, j) * jnp.exp(before[j]) for j in range(n_sub)], axis=0).astype(BF16)
        kh_scr[slot_of(blk, direction)] = jnp.concatenate(
            [sub(kd32, j) * jnp.exp(total - before[j] - t[j]) for j in range(n_sub)], axis=0).astype(BF16)

        lhs, where = [q_t], {}
        n_rows = GLA_BLOCK
        for d in range(2, n_sub):
            for j in scan[d:]:
                between = scan[at[j] - d + 1:at[j]]
                gap = t[between[0]]
                for i in between[1:]:
                    gap = gap + t[i]
                lhs.append((sub(q32, j) * jnp.exp(gap)).astype(BF16))
                where[(j, d)] = n_rows
                n_rows += c
        r = lax.dot_general(jnp.concatenate(lhs, axis=0), k_dec, contract_last, preferred_element_type=F32)
        a0 = lax.dot_general(q_t, k_t, contract_last, preferred_element_type=F32)
        att_rows = []
        for j in range(n_sub):
            att = jnp.where(masks0[j], sub(a0, j), 0.0)
            for d in range(1, at[j] + 1):
                src = sub(r, j) if d == 1 else r[where[(j, d)]:where[(j, d)] + c]
                att = jnp.where(ccol == scan[at[j] - d], src, att)
            att_rows.append(att)
        att = jnp.concatenate(att_rows, axis=0).astype(BF16)
        o_intra = _dot(att, v_ref[rs, :])
        if first:
            of_ref[rs, :] = o_intra
        else:
            of_ref[rs, :] += o_intra

    def stage_state(blk, direction, first):
        rs = rows_of(blk)
        s = s_ref[direction]
        o_inter = _dot(qh_scr[slot_of(blk, direction)], s.astype(BF16))
        upd = lax.dot_general(kh_scr[slot_of(blk, direction)], v_ref[rs, :], contract_first, preferred_element_type=F32)
        s_ref[direction] = dcol_scr[slot_of(blk, direction)] * s + upd
        if first:
            of_ref[rs, :] += o_inter
        else:
            o = of_ref[rs, :] + o_inter
            ms = jnp.mean(o * o, axis=-1, keepdims=True)
            y = (o * lax.rsqrt(ms + EPS)) * gnw_ref[...]
            gg = gg_ref[rs, :].astype(F32)
            o_ref[rs, :] = (y * (gg * jax.nn.sigmoid(gg))).astype(BF16)

    n = n_blocks
    if has_state:
        s_ref[0] = s0f_ref[...]
        s_ref[1] = s0b_ref[...]
    else:
        s_ref[...] = jnp.zeros_like(s_ref)

    def both(stage, i, *args):
        stage(i, 0, *args)
        stage(n - 1 - i, 1, *args)

    if n == 1:
        both(stage_decay, 0)
        stage_intra(0, 0, True)
        stage_intra(0, 1, False)
        stage_state(0, 0, True)
        stage_state(0, 1, False)
    else:
        half = n // 2
        both(stage_decay, 0)
        both(stage_decay, 1)
        both(stage_intra, 0, True)

        def first_half(i, carry):
            both(stage_state, i, True)
            both(stage_intra, i + 1, True)
            both(stage_decay, i + 2)
            return carry

        lax.fori_loop(0, half - 1, first_half, 0)
        both(stage_state, half - 1, True)
        both(stage_intra, half, False)
        both(stage_decay, half + 1)

        def second_half(i, carry):
            both(stage_state, i, False)
            both(stage_intra, i + 1, False)
            both(stage_decay, i + 2)
            return carry

        lax.fori_loop(half, n - 2, second_half, 0)
        both(stage_state, n - 2, False)
        both(stage_intra, n - 1, False)
        both(stage_state, n - 1, False)
    if emit_state:
        sf_ref[...] = s_ref[0]
        sb_ref[...] = s_ref[1]


def _gla(p3, a3, wz, bal, gnw, tri, s0f, s0b, emit_state, offsets, name):
    bsz, seq_len, _ = p3.shape
    dk = wz.shape[-1] // GLA_HEADS
    dv = gnw.shape[-1]
    off_q, off_k, off_v, off_gg = offsets
    assert off_q % dk == 0 and off_k % dk == 0 and off_v % dv == 0 and off_gg % dv == 0
    n_blocks = seq_len // GLA_BLOCK
    assert seq_len % GLA_BLOCK == 0 and GLA_BLOCK % GLA_CHUNK == 0
    assert n_blocks == 1 or (n_blocks % 2 == 0 and n_blocks >= 4)
    qb, kb, vb, ggb = off_q // dk, off_k // dk, off_v // dv, off_gg // dv
    has_state = s0f is not None
    in_specs = [
        pl.BlockSpec((None, seq_len, dk), lambda b, h: (b, 0, qb + h)),
        pl.BlockSpec((None, seq_len, dk), lambda b, h: (b, 0, kb + h)),
        pl.BlockSpec((None, seq_len, dv), lambda b, h: (b, 0, vb + h)),
        pl.BlockSpec((None, seq_len, dv), lambda b, h: (b, 0, ggb + h)),
        pl.BlockSpec((None, seq_len, LANES), lambda b, h: (b, 0, 0)),
        pl.BlockSpec((2, LANES, dk), lambda b, h: (0, 0, h)),
        pl.BlockSpec((2, 1, dk), lambda b, h: (0, 0, h)),
        pl.BlockSpec((None, 1, dv), lambda b, h: (h, 0, 0)),
        pl.BlockSpec((2, GLA_BLOCK, GLA_BLOCK), lambda b, h: (0, 0, 0)),
    ]
    args = [p3, p3, p3, p3, a3, wz, bal, gnw, tri]
    state_spec = pl.BlockSpec((None, None, None, dk, dv), lambda b, h: (b, 0, h, 0, 0))
    if has_state:
        in_specs += [state_spec, state_spec]
        args += [s0f, s0b]
    out_specs = [pl.BlockSpec((None, seq_len, dv), lambda b, h: (b, 0, h))]
    out_shape = [jax.ShapeDtypeStruct((bsz, seq_len, GLA_HEADS * dv), BF16)]
    if emit_state:
        out_specs += [state_spec, state_spec]
        out_shape += [jax.ShapeDtypeStruct((bsz, 1, GLA_HEADS, dk, dv), F32)] * 2
    return pl.pallas_call(
        functools.partial(_gla_kernel, seq_len=seq_len, has_state=has_state, emit_state=emit_state),
        grid=(bsz, GLA_HEADS),
        in_specs=in_specs,
        out_specs=out_specs,
        out_shape=out_shape,
        scratch_shapes=[
            pltpu.VMEM((seq_len, dv), F32),
            pltpu.VMEM((2, dk, dv), F32),
            pltpu.VMEM((2 * GLA_SLOTS, GLA_BLOCK, dk), F32),
            pltpu.VMEM((2 * GLA_SLOTS, dk, 1), F32),
            pltpu.VMEM((2 * GLA_SLOTS, GLA_BLOCK, dk), BF16),
            pltpu.VMEM((2 * GLA_SLOTS, GLA_BLOCK, dk), BF16),
        ],
        compiler_params=_params("parallel", "parallel"),
        name=name,
    )(*args)


def _fourier_ctx_kernel(u_ref, fg_ref, cs_ref, cl_ref, sl_ref, wf_ref, o_ref, re_ref):
    nb, seq_len, gc = u_ref.shape
    res = _dot(u_ref[...].reshape(nb * seq_len, gc), cs_ref[...])
    for i in range(nb):
        rs = slice(i * seq_len, (i + 1) * seq_len)
        uc = res[rs, :gc].astype(BF16)
        us = res[rs, gc:].astype(BF16)
        re_ref[rs, :] = (_dot(cl_ref[...], uc) - _dot(sl_ref[...], us)).astype(BF16)
    fg = fg_ref[...].reshape(nb * seq_len, gc).astype(F32)
    out = _dot(re_ref[...], wf_ref[...]) * (fg * jax.nn.sigmoid(fg))
    o_ref[...] = out.astype(BF16).reshape(nb, seq_len, gc)


def _fourier_ctx(p3, cs, cl, sl, wf, nb=4):
    bsz, seq_len, _ = p3.shape
    gc = wf.shape[-1]
    return pl.pallas_call(
        _fourier_ctx_kernel,
        grid=(bsz // nb, F_GROUPS),
        in_specs=[
            pl.BlockSpec((nb, seq_len, gc), lambda b, g: (b, 0, g)),
            pl.BlockSpec((nb, seq_len, gc), lambda b, g: (b, 0, F_GROUPS + g)),
            pl.BlockSpec((gc, 2 * gc), lambda b, g: (0, 0)),
            pl.BlockSpec((seq_len, seq_len), lambda b, g: (0, 0)),
            pl.BlockSpec((seq_len, seq_len), lambda b, g: (0, 0)),
            pl.BlockSpec((None, gc, gc), lambda b, g: (g, 0, 0)),
        ],
        out_specs=pl.BlockSpec((nb, seq_len, gc), lambda b, g: (b, 0, g)),
        out_shape=jax.ShapeDtypeStruct((bsz, seq_len, F_GROUPS * gc), BF16),
        scratch_shapes=[pltpu.VMEM((nb * seq_len, gc), BF16)],
        compiler_params=_params("parallel", "parallel"),
        name="fourier_ctx",
    )(p3, p3, cs, cl, sl, wf)


COL_TILE = 16
ROW_RADIX = 4


def _fourier_lat1_kernel(u_ref, mr_ref, mc_ref, y_ref):
    rows, cw, gc = u_ref.shape
    lo = rows // ROW_RADIX
    q = lo * cw
    u = [u_ref[lo * h:lo * (h + 1)].reshape(q, gc).astype(F32) for h in range(ROW_RADIX)]
    s02, s13, a, b = u[0] + u[2], u[1] + u[3], u[0] - u[2], u[1] - u[3]
    v0 = (s02 + s13).astype(BF16)
    v2 = (s02 - s13).astype(BF16)
    ab = jnp.concatenate([a.astype(BF16), b.astype(BF16)], axis=0)
    outs = {0: _dot(mr_ref[0], v0), 2: _dot(mr_ref[1], v2), 1: _dot(mc_ref[0], ab), 3: _dot(mc_ref[1], ab)}
    for k1, out in outs.items():
        o = out.astype(BF16)
        for part in range(2):
            blk = o[part * q:(part + 1) * q].reshape(lo, cw, gc)
            for k2 in range(lo):
                y_ref[part, k1 + ROW_RADIX * k2] = blk[k2]


def _row_dft_mats(n, tile):
    lo = n // ROW_RADIX
    eye = np.eye(tile)
    r_lo = np.arange(lo)
    mats = {}
    for k1 in range(ROW_RADIX):
        r_out = k1 + ROW_RADIX * np.arange(lo)
        ang = 2.0 * np.pi * ((r_out[:, None] * r_lo[None, :]) % n) / n
        mats[k1] = (np.kron(np.cos(ang) / np.sqrt(n), eye), np.kron(np.sin(ang) / np.sqrt(n), eye))
    real = np.stack([np.concatenate([mats[k][0], -mats[k][1]], axis=0) for k in (0, 2)])
    (c1, s1), (c3, s3) = mats[1], mats[3]
    cplx = np.stack([np.block([[c1, -s1], [-s1, -c1]]), np.block([[c3, s3], [-s3, c3]])])
    return real.astype(np.float32), cplx.astype(np.float32)


def _fourier_lat1(p4, mr, mc, gc):
    bsz, rows, cols, _ = p4.shape
    q = rows // ROW_RADIX * COL_TILE
    return pl.pallas_call(
        _fourier_lat1_kernel,
        grid=(bsz, cols // COL_TILE, F_GROUPS),
        in_specs=[
            pl.BlockSpec((None, rows, COL_TILE, gc), lambda b, cb, g: (b, 0, cb, g)),
            pl.BlockSpec((2, 2 * q, q), lambda b, cb, g: (0, 0, 0)),
            pl.BlockSpec((2, 2 * q, 2 * q), lambda b, cb, g: (0, 0, 0)),
        ],
        out_specs=pl.BlockSpec((None, None, 2, rows, COL_TILE, gc), lambda b, cb, g: (b, g, 0, 0, cb, 0)),
        out_shape=jax.ShapeDtypeStruct((bsz, F_GROUPS, 2, rows, cols, gc), BF16),
        compiler_params=_params("parallel", "parallel", "parallel"),
        name="fourier_lat1",
    )(p4, mr, mc)


ROW_TILE = 16


def _fourier_lat2_kernel(y_ref, fg_ref, m2_ref, cs_ref, wf_ref, o_ref, zz_ref):
    _, rt, cols, gc = y_ref.shape
    sub = 2 * cols
    for s in range(rt // 2):
        rs = slice(s * sub, (s + 1) * sub)
        st = jnp.concatenate([y_ref[0, 2 * s:2 * s + 2].reshape(sub, gc),
                              y_ref[1, 2 * s:2 * s + 2].reshape(sub, gc)], axis=0)
        out = _dot(m2_ref[...], st)
        zz_ref[rs, :gc] = out[:sub].astype(BF16)
        zz_ref[rs, gc:] = out[sub:].astype(BF16)
    re = _dot(zz_ref[...], cs_ref[...])
    fg = fg_ref[...].astype(F32)
    o_ref[...] = (_dot(re.astype(BF16), wf_ref[...]) * (fg * jax.nn.sigmoid(fg))).astype(BF16)


def _fourier_lat2(y6, p2, m2, cs_stack, wf):
    bsz, _, _, rows, cols, gc = y6.shape
    tokens = ROW_TILE * cols
    tiles = rows // ROW_TILE
    return pl.pallas_call(
        _fourier_lat2_kernel,
        grid=(bsz, tiles, F_GROUPS),
        in_specs=[
            pl.BlockSpec((None, None, 2, ROW_TILE, cols, gc), lambda b, rb, g: (b, g, 0, rb, 0, 0)),
            pl.BlockSpec((tokens, gc), lambda b, rb, g: (b * tiles + rb, F_GROUPS + g)),
            pl.BlockSpec((4 * cols, 4 * cols), lambda b, rb, g: (0, 0)),
            pl.BlockSpec((2 * gc, gc), lambda b, rb, g: (0, 0)),
            pl.BlockSpec((None, gc, gc), lambda b, rb, g: (g, 0, 0)),
        ],
        out_specs=pl.BlockSpec((tokens, gc), lambda b, rb, g: (b * tiles + rb, g)),
        out_shape=jax.ShapeDtypeStruct((bsz * rows * cols, F_GROUPS * gc), BF16),
        scratch_shapes=[pltpu.VMEM((tokens, 2 * gc), BF16)],
        compiler_params=_params("parallel", "parallel", "parallel"),
        name="fourier_lat2",
    )(y6, p2, m2, cs_stack, wf)


OUT_COL_CHUNK = 1024


def _outproj_kernel(f_ref, g_ref, x_ref, mod_ref, fnw_ref, w_ref, o_ref):
    tm, fw = f_ref.shape
    d = w_ref.shape[-1]
    f = f_ref[...]
    g = g_ref[...]
    ss = jnp.zeros((tm, 1), F32)
    for c0 in range(0, d, OUT_COL_CHUNK):
        cs = slice(c0, c0 + OUT_COL_CHUNK)
        acc = _dot(f, w_ref[:fw, cs]) + _dot(g, w_ref[fw:, cs])
        pre = x_ref[:, cs] + mod_ref[2:3, cs] * acc
        ss = ss + jnp.sum(pre * pre, axis=-1, keepdims=True)
        o_ref[:, cs] = pre
    scale = lax.rsqrt(ss * (1.0 / d) + EPS)
    for c0 in range(0, d, OUT_COL_CHUNK):
        cs = slice(c0, c0 + OUT_COL_CHUNK)
        o_ref[:, cs] = (o_ref[:, cs] * scale) * fnw_ref[:, cs]


def _outproj(f2, g2, x2, mod3, fnw, w_bf, mod_row, name, tm=256):
    t, d = x2.shape
    fw = f2.shape[-1]
    return pl.pallas_call(
        _outproj_kernel,
        grid=(t // tm,),
        in_specs=[
            pl.BlockSpec((tm, fw), lambda i: (i, 0)),
            pl.BlockSpec((tm, d - fw), lambda i: (i, 0)),
            pl.BlockSpec((tm, d), lambda i: (i, 0)),
            pl.BlockSpec((None, 3, d), lambda i: (mod_row(i), 0, 0)),
            pl.BlockSpec((1, d), lambda i: (0, 0)),
            pl.BlockSpec((d, d), lambda i: (0, 0), pipeline_mode=pl.Buffered(1)),
        ],
        out_specs=pl.BlockSpec((tm, d), lambda i: (i, 0)),
        out_shape=jax.ShapeDtypeStruct((t, d), F32),
        compiler_params=_params("parallel"),
        name=name,
    )(f2, g2, x2, mod3, fnw, w_bf)


def kernel(x_prompt, x_sample, state_gla_fwd, state_gla_bwd, c, c_ctx, ada_w, ada_b, norm_w,
           w_in, w_alpha, b_alpha, w_fourier, gla_norm_w, w_out, final_norm_w):
    bp, lp, d = x_prompt.shape
    bs, ls, _ = x_sample.shape
    depth = ada_w.shape[0]
    assert depth == 1, "single-layer step"
    fw = d // 2
    gc = fw // F_GROUPS
    dk_total = w_alpha.shape[-1]
    n_main = 2 * fw + 2 * dk_total + 2 * (d - fw)
    gla_offsets = (2 * fw, 2 * fw + dk_total, 2 * fw + 2 * dk_total, 2 * fw + 2 * dk_total + (d - fw))
    rows = ls // GRID_W
    assert bs + 1 <= MOD_ROWS and rows == GRID_W and 8 * GLA_LOWRANK == LANES

    cc = jnp.zeros((MOD_ROWS, d), F32).at[:bs].set(c).at[bs].set(c_ctx)
    mod3 = _modulation(cc, ada_w[0], ada_b).reshape(MOD_ROWS, 3, d)

    w_all = w_in[0].astype(BF16)
    w_a = jnp.tile(w_in[0, :, n_main:], (1, LANES // (2 * GLA_LOWRANK))).astype(BF16)
    w_out_bf = w_out[0].astype(BF16)
    wf_bf = w_fourier[0].astype(BF16)
    nw = norm_w[0][None, :]
    fnw = final_norm_w[None, :]
    lr = GLA_LOWRANK
    wdir = jnp.zeros((2, 2 * lr, dk_total), F32).at[0, :lr].set(w_alpha[0, 0]).at[1, lr:].set(w_alpha[0, 1])
    wdir_hi = wdir.astype(BF16)
    wdir_lo = (wdir - wdir_hi.astype(F32)).astype(BF16)
    wz = jnp.concatenate([wdir_hi, wdir_hi, wdir_lo, jnp.zeros_like(wdir_hi)], axis=1)
    bal = b_alpha[0][:, None, :]
    gnw = gla_norm_w[0][:, None, :]
    pos_in_block = np.arange(GLA_BLOCK)
    same_chunk = (pos_in_block[:, None] // GLA_CHUNK) == (pos_in_block[None, :] // GLA_CHUNK)
    lower = pos_in_block[:, None] >= pos_in_block[None, :]
    tri = jnp.asarray(np.stack([same_chunk & lower, same_chunk & lower.T]).astype(np.float32)).astype(BF16)

    cc_c, cc_s = _dft_mats(gc)
    cs = jnp.asarray(np.concatenate([cc_c, cc_s], axis=1)).astype(BF16)
    cs_stack = jnp.asarray(np.concatenate([cc_c, cc_s], axis=0)).astype(BF16)
    cl_c, cl_s = _dft_mats(lp)
    cl = jnp.asarray(cl_c).astype(BF16)
    sl = jnp.asarray(cl_s).astype(BF16)
    g_c, g_s = _dft_mats(GRID_W)
    eye2 = np.eye(2, dtype=np.float32)
    a2, b2 = np.kron(eye2, g_c), np.kron(eye2, g_s)
    m2 = jnp.asarray(np.block([[a2, b2], [-b2, a2]])).astype(BF16)
    row_real, row_cplx = _row_dft_mats(GRID_W, COL_TILE)
    mr, mc = jnp.asarray(row_real).astype(BF16), jnp.asarray(row_cplx).astype(BF16)

    xp2 = x_prompt.reshape(bp * lp, d)
    ctx_row = lambda i: bs
    p_ctx, a_ctx = _inproj(xp2, mod3, nw, w_all, w_a, n_main, ctx_row, "inproj_ctx")
    p3 = p_ctx.reshape(bp, lp, n_main)
    g_ctx, new_f, new_b = _gla(p3, a_ctx.reshape(bp, lp, LANES), wz, bal, gnw, tri,
                               None, None, True, gla_offsets, "gla_ctx")
    f_ctx = _fourier_ctx(p3, cs, cl, sl, wf_bf)
    y_prompt = _outproj(f_ctx.reshape(bp * lp, fw), g_ctx.reshape(bp * lp, d - fw),
                        xp2, mod3, fnw, w_out_bf, ctx_row, "outproj_ctx")

    xs2 = x_sample.reshape(bs * ls, d)
    tm_i = 1024
    p_lat, a_lat = _inproj(xs2, mod3, nw, w_all, w_a, n_main,
                           lambda i: i // (ls // tm_i), "inproj_lat", tm=tm_i)
    (g_lat,) = _gla(p_lat.reshape(bs, ls, n_main), a_lat.reshape(bs, ls, LANES), wz, bal, gnw, tri,
                    state_gla_fwd, state_gla_bwd, False, gla_offsets, "gla_lat")
    y6 = _fourier_lat1(p_lat.reshape(bs, rows, GRID_W, n_main), mr, mc, gc)
    f_lat = _fourier_lat2(y6, p_lat, m2, cs_stack, wf_bf)
    tm_o = 256
    y_sample = _outproj(f_lat, g_lat.reshape(bs * ls, d - fw), xs2, mod3, fnw,
                        w_out_bf, lambda i: i // (ls // tm_o), "outproj_lat", tm=tm_o)

    return (y_prompt.reshape(bp, lp, d), y_sample.reshape(bs, ls, d), new_f, new_b)
```

```python
import functools

import numpy as np
import jax
import jax.numpy as jnp
from jax import lax
from jax.experimental import pallas as pl
from jax.experimental.pallas import tpu as pltpu

F32 = jnp.float32
BF16 = jnp.bfloat16

GRID_W = 64
F_GROUPS = 4
GLA_HEADS = 4
GLA_LOWRANK = 16
GLA_TAU = 16.0
GLA_CHUNK = 64
GLA_BLOCK = 256
GLA_SLOTS = 4
EPS = 1e-6

VMEM_LIMIT_BYTES = 58 * 1024 * 1024
LANES = 128
MOD_ROWS = 16


def _params(*sem):
    return pltpu.CompilerParams(dimension_semantics=sem, vmem_limit_bytes=VMEM_LIMIT_BYTES)


def _dot(a, b):
    return jnp.dot(a, b, preferred_element_type=F32)


def _split2(x):
    hi = x.astype(BF16)
    lo = (x - hi.astype(F32)).astype(BF16)
    return hi, lo


def _dft_mats(n):
    k = np.arange(n)
    ang = 2.0 * np.pi * ((k[:, None] * k[None, :]) % n) / n
    s = 1.0 / np.sqrt(n)
    return (np.cos(ang) * s).astype(np.float32), (np.sin(ang) * s).astype(np.float32)


def _mod_kernel(c_ref, w_ref, b_ref, o_ref):
    c = c_ref[...]
    s = c * jax.nn.sigmoid(c)
    s_hi, s_lo = _split2(s)
    w_hi, w_lo = _split2(w_ref[...])
    o_ref[...] = _dot(s_hi, w_hi) + _dot(s_hi, w_lo) + _dot(s_lo, w_hi) + b_ref[...]


def _modulation(cc, ada_w, ada_b, tn=512):
    d, n = ada_w.shape
    return pl.pallas_call(
        _mod_kernel,
        grid=(n // tn,),
        in_specs=[
            pl.BlockSpec((MOD_ROWS, d), lambda j: (0, 0)),
            pl.BlockSpec((d, tn), lambda j: (0, j)),
            pl.BlockSpec((1, tn), lambda j: (0, j)),
        ],
        out_specs=pl.BlockSpec((MOD_ROWS, tn), lambda j: (0, j)),
        out_shape=jax.ShapeDtypeStruct((MOD_ROWS, n), F32),
        compiler_params=_params("parallel"),
        name="mod",
    )(cc, ada_w, ada_b)


def _inproj_kernel(x_ref, mod_ref, nw_ref, w_ref, wa_ref, p_ref, a_ref, h0_ref, h1_ref, *, n_slices):
    i = pl.program_id(0)
    j = pl.program_id(1)
    rows = x_ref.shape[0]
    odd = i % 2

    def step(h_cur, h_prev, first):
        if not first:
            p_ref[...] = _dot(h_prev[...], w_ref[...]).astype(BF16)
        x = x_ref[...]
        ms = jnp.mean(x * x, axis=-1, keepdims=True)
        y = (x * lax.rsqrt(ms + EPS)) * nw_ref[...]
        h = y * (1.0 + mod_ref[1:2, :]) + mod_ref[0:1, :]
        start = pl.multiple_of(jnp.minimum(j, n_slices - 1) * rows, rows)
        h_cur[pl.ds(start, rows), :] = h.astype(BF16)
        if first:
            return

        @pl.when(j == 0)
        def _():
            a_ref[...] = _dot(h_prev[...], wa_ref[...])

    @pl.when(i == 0)
    def _():
        step(h0_ref, h1_ref, True)

    @pl.when((i > 0) & (odd == 0))
    def _():
        step(h0_ref, h1_ref, False)

    @pl.when(odd == 1)
    def _():
        step(h1_ref, h0_ref, False)


def _inproj(x2, mod3, nw, w_bf, wa_bf, n_main, mod_row, name, tm=1024, tn=512, n_slices=16):
    t, d = x2.shape
    n_i, n_j = t // tm, n_main // tn
    rows = tm // n_slices
    assert n_slices <= n_j and rows % 8 == 0
    tile = lambda i: jnp.minimum(i, n_i - 1)
    done = lambda i: jnp.maximum(i - 1, 0)
    col = lambda i, j: jnp.where(i == 0, 0, j)
    return pl.pallas_call(
        functools.partial(_inproj_kernel, n_slices=n_slices),
        grid=(n_i + 1, n_j),
        in_specs=[
            pl.BlockSpec((rows, d), lambda i, j: (tile(i) * n_slices + jnp.minimum(j, n_slices - 1), 0)),
            pl.BlockSpec((None, 3, d), lambda i, j: (mod_row(tile(i)), 0, 0)),
            pl.BlockSpec((1, d), lambda i, j: (0, 0)),
            pl.BlockSpec((d, tn), lambda i, j: (0, col(i, j))),
            pl.BlockSpec((d, LANES), lambda i, j: (0, 0)),
        ],
        out_specs=[
            pl.BlockSpec((tm, tn), lambda i, j: (done(i), col(i, j))),
            pl.BlockSpec((tm, LANES), lambda i, j: (done(i), 0)),
        ],
        out_shape=[
            jax.ShapeDtypeStruct((t, n_main), BF16),
            jax.ShapeDtypeStruct((t, LANES), F32),
        ],
        scratch_shapes=[pltpu.VMEM((tm, d), BF16), pltpu.VMEM((tm, d), BF16)],
        compiler_params=_params("arbitrary", "arbitrary"),
        name=name,
    )(x2, mod3, nw, w_bf, wa_bf)


def _gla_kernel(*refs, seq_len, has_state, emit_state):
    q_ref, k_ref, v_ref, gg_ref, a_ref, wz_ref, bal_ref, gnw_ref, tri_ref = refs[:9]
    pos = 9
    if has_state:
        s0f_ref, s0b_ref = refs[pos:pos + 2]
        pos += 2
    o_ref = refs[pos]
    pos += 1
    if emit_state:
        sf_ref, sb_ref = refs[pos:pos + 2]
        pos += 2
    of_ref, s_ref, b_scr, dcol_scr, qh_scr, kh_scr = refs[pos:pos + 6]

    c = GLA_CHUNK
    n_sub = GLA_BLOCK // c
    n_blocks = seq_len // GLA_BLOCK
    dk = q_ref.shape[-1]
    q_scale = float(dk) ** -0.5
    lrow = lax.broadcasted_iota(jnp.int32, (c, GLA_BLOCK), 0)
    bcol = lax.broadcasted_iota(jnp.int32, (c, GLA_BLOCK), 1)
    ccol = bcol // c
    lcol = bcol - ccol * c
    a_group = lax.broadcasted_iota(jnp.int32, (GLA_BLOCK, LANES), 1) // (2 * GLA_LOWRANK)
    contract_last = (((1,), (1,)), ((), ()))
    contract_first = (((0,), (0,)), ((), ()))
    sub = lambda x, j: x[j * c:(j + 1) * c]

    def rows_of(blk):
        start = blk * GLA_BLOCK
        if not isinstance(blk, int):
            start = pl.multiple_of(start, GLA_BLOCK)
        return pl.ds(start, GLA_BLOCK)

    def slot_of(blk, direction):
        return direction * GLA_SLOTS + blk % GLA_SLOTS

    def chunk_masks(fwd):
        return [(ccol == j) & ((lrow >= lcol) if fwd else (lcol >= lrow)) for j in range(n_sub)]

    def stage_decay(blk, direction, state_decays=True):
        a = a_ref[rows_of(blk), :]
        hi = a.astype(BF16).astype(F32)
        lhs = jnp.where(a_group == 1, a - hi, jnp.where(a_group == 3, 0.0, hi)).astype(BF16)
        z = _dot(lhs, wz_ref[direction]) + bal_ref[direction]
        g = (jnp.minimum(z, 0.0) - jnp.log(1.0 + jnp.exp(-jnp.abs(z)))) * (1.0 / GLA_TAU)
        tri = tri_ref[direction]
        g_hi, g_lo = _split2(g)
        b_scr[slot_of(blk, direction)] = _dot(tri, g_hi) + _dot(tri, g_lo)
        if state_decays:
            dcol_scr[slot_of(blk, direction)] = jnp.exp(jnp.sum(g.T, axis=-1, keepdims=True))

    def stage_intra(blk, direction, first):
        fwd = direction == 0
        rs = rows_of(blk)
        scan = list(range(n_sub)) if fwd else list(range(n_sub - 1, -1, -1))
        at = {j: p for p, j in enumerate(scan)}
        masks0 = chunk_masks(fwd)
        b = b_scr[slot_of(blk, direction)]
        t = [sub(b, j)[c - 1:c, :] if fwd else sub(b, j)[0:1, :] for j in range(n_sub)]
        b_end = jnp.concatenate([jnp.broadcast_to(t[j], (c, dk)) for j in range(n_sub)], axis=0)
        q = q_ref[rs, :].astype(F32)
        k = k_ref[rs, :].astype(F32)
        q32 = q * q_scale * jnp.exp(b)
        q_t = q32.astype(BF16)
        k_t = (k * jnp.exp(-b)).astype(BF16)
        kd32 = k * jnp.exp(b_end - b)
        k_dec = kd32.astype(BF16)

        before, acc = {}, jnp.zeros_like(t[0])
        for j in scan:
            before[j] = acc
            acc = acc + t[j]
        total = acc
        qh_scr[slot_of(blk, direction)] = jnp.concatenate(
            [sub(q32, j) * jnp.exp(before[j]) for j in range(n_sub)], axis=0).astype(BF16)
        kh_scr[slot_of(blk, direction)] = jnp.concatenate(
            [sub(kd32, j) * jnp.exp(total - before[j] - t[j]) for j in range(n_sub)], axis=0).astype(BF16)

        lhs, where = [q_t], {}
        n_rows = GLA_BLOCK
        for d in range(2, n_sub):
            for j in scan[d:]:
                between = scan[at[j] - d + 1:at[j]]
                gap = t[between[0]]
                for i in between[1:]:
                    gap = gap + t[i]
                lhs.append((sub(q32, j) * jnp.exp(gap)).astype(BF16))
                where[(j, d)] = n_rows
                n_rows += c
        r = lax.dot_general(jnp.concatenate(lhs, axis=0), k_dec, contract_last, preferred_element_type=F32)
        a0 = lax.dot_general(q_t, k_t, contract_last, preferred_element_type=F32)
        att_rows = []
        for j in range(n_sub):
            att = jnp.where(masks0[j], sub(a0, j), 0.0)
            for d in range(1, at[j] + 1):
                src = sub(r, j) if d == 1 else r[where[(j, d)]:where[(j, d)] + c]
                att = jnp.where(ccol == scan[at[j] - d], src, att)
            att_rows.append(att)
        att = jnp.concatenate(att_rows, axis=0).astype(BF16)
        o_intra = _dot(att, v_ref[rs, :])
        if first:
            of_ref[rs, :] = o_intra
        else:
            of_ref[rs, :] += o_intra

    def stage_state(blk, direction, first, from_zero=False):
        rs = rows_of(blk)
        upd = lax.dot_general(kh_scr[slot_of(blk, direction)], v_ref[rs, :], contract_first, preferred_element_type=F32)
        if from_zero:
            o_inter = jnp.zeros((GLA_BLOCK, v_ref.shape[-1]), F32)
            s_ref[direction] = upd
        else:
            s = s_ref[direction]
            o_inter = _dot(qh_scr[slot_of(blk, direction)], s.astype(BF16))
            s_ref[direction] = dcol_scr[slot_of(blk, direction)] * s + upd
        if first:
            of_ref[rs, :] += o_inter
        else:
            o = of_ref[rs, :] + o_inter
            ms = jnp.mean(o * o, axis=-1, keepdims=True)
            y = (o * lax.rsqrt(ms + EPS)) * gnw_ref[...]
            gg = gg_ref[rs, :].astype(F32)
            o_ref[rs, :] = (y * (gg * jax.nn.sigmoid(gg))).astype(BF16)

    n = n_blocks
    if has_state:
        s_ref[0] = s0f_ref[...]
        s_ref[1] = s0b_ref[...]
    else:
        s_ref[...] = jnp.zeros_like(s_ref)

    def both(stage, i, *args):
        stage(i, 0, *args)
        stage(n - 1 - i, 1, *args)

    if n == 1:
        both(stage_decay, 0, has_state)
        stage_intra(0, 0, True)
        stage_intra(0, 1, False)
        stage_state(0, 0, True, not has_state)
        stage_state(0, 1, False, not has_state)
    else:
        half = n // 2
        both(stage_decay, 0)
        both(stage_decay, 1)
        both(stage_intra, 0, True)

        def first_half(i, carry):
            both(stage_state, i, True)
            both(stage_intra, i + 1, True)
            both(stage_decay, i + 2)
            return carry

        lax.fori_loop(0, half - 1, first_half, 0)
        both(stage_state, half - 1, True)
        both(stage_intra, half, False)
        both(stage_decay, half + 1)

        def second_half(i, carry):
            both(stage_state, i, False)
            both(stage_intra, i + 1, False)
            both(stage_decay, i + 2)
            return carry

        lax.fori_loop(half, n - 2, second_half, 0)
        both(stage_state, n - 2, False)
        both(stage_intra, n - 1, False)
        both(stage_state, n - 1, False)
    if emit_state:
        sf_ref[...] = s_ref[0]
        sb_ref[...] = s_ref[1]


def _gla(p3, a3, wz, bal, gnw, tri, s0f, s0b, emit_state, offsets, name):
    bsz, seq_len, _ = p3.shape
    dk = wz.shape[-1] // GLA_HEADS
    dv = gnw.shape[-1]
    off_q, off_k, off_v, off_gg = offsets
    assert off_q % dk == 0 and off_k % dk == 0 and off_v % dv == 0 and off_gg % dv == 0
    n_blocks = seq_len // GLA_BLOCK
    assert seq_len % GLA_BLOCK == 0 and GLA_BLOCK % GLA_CHUNK == 0
    assert n_blocks == 1 or (n_blocks % 2 == 0 and n_blocks >= 4)
    qb, kb, vb, ggb = off_q // dk, off_k // dk, off_v // dv, off_gg // dv
    has_state = s0f is not None
    in_specs = [
        pl.BlockSpec((None, seq_len, dk), lambda b, h: (b, 0, qb + h)),
        pl.BlockSpec((None, seq_len, dk), lambda b, h: (b, 0, kb + h)),
        pl.BlockSpec((None, seq_len, dv), lambda b, h: (b, 0, vb + h)),
        pl.BlockSpec((None, seq_len, dv), lambda b, h: (b, 0, ggb + h)),
        pl.BlockSpec((None, seq_len, LANES), lambda b, h: (b, 0, 0)),
        pl.BlockSpec((2, LANES, dk), lambda b, h: (0, 0, h)),
        pl.BlockSpec((2, 1, dk), lambda b, h: (0, 0, h)),
        pl.BlockSpec((None, 1, dv), lambda b, h: (h, 0, 0)),
        pl.BlockSpec((2, GLA_BLOCK, GLA_BLOCK), lambda b, h: (0, 0, 0)),
    ]
    args = [p3, p3, p3, p3, a3, wz, bal, gnw, tri]
    state_spec = pl.BlockSpec((None, None, None, dk, dv), lambda b, h: (b, 0, h, 0, 0))
    if has_state:
        in_specs += [state_spec, state_spec]
        args += [s0f, s0b]
    out_specs = [pl.BlockSpec((None, seq_len, dv), lambda b, h: (b, 0, h))]
    out_shape = [jax.ShapeDtypeStruct((bsz, seq_len, GLA_HEADS * dv), BF16)]
    if emit_state:
        out_specs += [state_spec, state_spec]
        out_shape += [jax.ShapeDtypeStruct((bsz, 1, GLA_HEADS, dk, dv), F32)] * 2
    return pl.pallas_call(
        functools.partial(_gla_kernel, seq_len=seq_len, has_state=has_state, emit_state=emit_state),
        grid=(bsz, GLA_HEADS),
        in_specs=in_specs,
        out_specs=out_specs,
        out_shape=out_shape,
        scratch_shapes=[
            pltpu.VMEM((seq_len, dv), F32),
            pltpu.VMEM((2, dk, dv), F32),
            pltpu.VMEM((2 * GLA_SLOTS, GLA_BLOCK, dk), F32),
            pltpu.VMEM((2 * GLA_SLOTS, dk, 1), F32),
            pltpu.VMEM((2 * GLA_SLOTS, GLA_BLOCK, dk), BF16),
            pltpu.VMEM((2 * GLA_SLOTS, GLA_BLOCK, dk), BF16),
        ],
        compiler_params=_params("parallel", "parallel"),
        name=name,
    )(*args)


def _fourier_ctx_kernel(u_ref, fg_ref, cs_ref, cl_ref, sl_ref, wf_ref, o_ref, re_ref):
    nb, seq_len, gc = u_ref.shape
    res = _dot(u_ref[...].reshape(nb * seq_len, gc), cs_ref[...])
    for i in range(nb):
        rs = slice(i * seq_len, (i + 1) * seq_len)
        uc = res[rs, :gc].astype(BF16)
        us = res[rs, gc:].astype(BF16)
        re_ref[rs, :] = (_dot(cl_ref[...], uc) - _dot(sl_ref[...], us)).astype(BF16)
    fg = fg_ref[...].reshape(nb * seq_len, gc).astype(F32)
    out = _dot(re_ref[...], wf_ref[...]) * (fg * jax.nn.sigmoid(fg))
    o_ref[...] = out.astype(BF16).reshape(nb, seq_len, gc)


def _fourier_ctx(p3, cs, cl, sl, wf, nb=4):
    bsz, seq_len, _ = p3.shape
    gc = wf.shape[-1]
    return pl.pallas_call(
        _fourier_ctx_kernel,
        grid=(bsz // nb, F_GROUPS),
        in_specs=[
            pl.BlockSpec((nb, seq_len, gc), lambda b, g: (b, 0, g)),
            pl.BlockSpec((nb, seq_len, gc), lambda b, g: (b, 0, F_GROUPS + g)),
            pl.BlockSpec((gc, 2 * gc), lambda b, g: (0, 0)),
            pl.BlockSpec((seq_len, seq_len), lambda b, g: (0, 0)),
            pl.BlockSpec((seq_len, seq_len), lambda b, g: (0, 0)),
            pl.BlockSpec((None, gc, gc), lambda b, g: (g, 0, 0)),
        ],
        out_specs=pl.BlockSpec((nb, seq_len, gc), lambda b, g: (b, 0, g)),
        out_shape=jax.ShapeDtypeStruct((bsz, seq_len, F_GROUPS * gc), BF16),
        scratch_shapes=[pltpu.VMEM((nb * seq_len, gc), BF16)],
        compiler_params=_params("parallel", "parallel"),
        name="fourier_ctx",
    )(p3, p3, cs, cl, sl, wf)


COL_TILE = 16
ROW_RADIX = 4


def _fourier_lat1_kernel(u_ref, mr_ref, mc_ref, y_ref):
    rows, cw, gc = u_ref.shape
    lo = rows // ROW_RADIX
    q = lo * cw
    u = [u_ref[lo * h:lo * (h + 1)].reshape(q, gc).astype(F32) for h in range(ROW_RADIX)]
    s02, s13, a, b = u[0] + u[2], u[1] + u[3], u[0] - u[2], u[1] - u[3]
    v0 = (s02 + s13).astype(BF16)
    v2 = (s02 - s13).astype(BF16)
    ab = jnp.concatenate([a.astype(BF16), b.astype(BF16)], axis=0)
    outs = {0: _dot(mr_ref[0], v0), 2: _dot(mr_ref[1], v2), 1: _dot(mc_ref[0], ab), 3: _dot(mc_ref[1], ab)}
    for k1, out in outs.items():
        o = out.astype(BF16)
        for part in range(2):
            blk = o[part * q:(part + 1) * q].reshape(lo, cw, gc)
            for k2 in range(lo):
                y_ref[part, k1 + ROW_RADIX * k2] = blk[k2]


def _row_dft_mats(n, tile):
    lo = n // ROW_RADIX
    eye = np.eye(tile)
    r_lo = np.arange(lo)
    mats = {}
    for k1 in range(ROW_RADIX):
        r_out = k1 + ROW_RADIX * np.arange(lo)
        ang = 2.0 * np.pi * ((r_out[:, None] * r_lo[None, :]) % n) / n
        mats[k1] = (np.kron(np.cos(ang) / np.sqrt(n), eye), np.kron(np.sin(ang) / np.sqrt(n), eye))
    real = np.stack([np.concatenate([mats[k][0], -mats[k][1]], axis=0) for k in (0, 2)])
    (c1, s1), (c3, s3) = mats[1], mats[3]
    cplx = np.stack([np.block([[c1, -s1], [-s1, -c1]]), np.block([[c3, s3], [-s3, c3]])])
    return real.astype(np.float32), cplx.astype(np.float32)


def _fourier_lat1(p4, mr, mc, gc):
    bsz, rows, cols, _ = p4.shape
    q = rows // ROW_RADIX * COL_TILE
    return pl.pallas_call(
        _fourier_lat1_kernel,
        grid=(bsz, cols // COL_TILE, F_GROUPS),
        in_specs=[
            pl.BlockSpec((None, rows, COL_TILE, gc), lambda b, cb, g: (b, 0, cb, g)),
            pl.BlockSpec((2, 2 * q, q), lambda b, cb, g: (0, 0, 0)),
            pl.BlockSpec((2, 2 * q, 2 * q), lambda b, cb, g: (0, 0, 0)),
        ],
        out_specs=pl.BlockSpec((None, None, 2, rows, COL_TILE, gc), lambda b, cb, g: (b, g, 0, 0, cb, 0)),
        out_shape=jax.ShapeDtypeStruct((bsz, F_GROUPS, 2, rows, cols, gc), BF16),
        compiler_params=_params("parallel", "parallel", "parallel"),
        name="fourier_lat1",
    )(p4, mr, mc)


ROW_TILE = 32


def _fourier_lat2_kernel(y_ref, fg_ref, m2_ref, cs_ref, wf_ref, o_ref, zz_ref):
    _, rt, cols, gc = y_ref.shape
    sub = 2 * cols
    for s in range(rt // 2):
        rs = slice(s * sub, (s + 1) * sub)
        st = jnp.concatenate([y_ref[0, 2 * s:2 * s + 2].reshape(sub, gc),
                              y_ref[1, 2 * s:2 * s + 2].reshape(sub, gc)], axis=0)
        out = _dot(m2_ref[...], st)
        zz_ref[rs, :gc] = out[:sub].astype(BF16)
        zz_ref[rs, gc:] = out[sub:].astype(BF16)
    re = _dot(zz_ref[...], cs_ref[...])
    fg = fg_ref[...].astype(F32)
    o_ref[...] = (_dot(re.astype(BF16), wf_ref[...]) * (fg * jax.nn.sigmoid(fg))).astype(BF16)


def _fourier_lat2(y6, p2, m2, cs_stack, wf):
    bsz, _, _, rows, cols, gc = y6.shape
    tokens = ROW_TILE * cols
    tiles = rows // ROW_TILE
    return pl.pallas_call(
        _fourier_lat2_kernel,
        grid=(bsz, tiles, F_GROUPS),
        in_specs=[
            pl.BlockSpec((None, None, 2, ROW_TILE, cols, gc), lambda b, rb, g: (b, g, 0, rb, 0, 0)),
            pl.BlockSpec((tokens, gc), lambda b, rb, g: (b * tiles + rb, F_GROUPS + g)),
            pl.BlockSpec((4 * cols, 4 * cols), lambda b, rb, g: (0, 0)),
            pl.BlockSpec((2 * gc, gc), lambda b, rb, g: (0, 0)),
            pl.BlockSpec((None, gc, gc), lambda b, rb, g: (g, 0, 0)),
        ],
        out_specs=pl.BlockSpec((tokens, gc), lambda b, rb, g: (b * tiles + rb, g)),
        out_shape=jax.ShapeDtypeStruct((bsz * rows * cols, F_GROUPS * gc), BF16),
        scratch_shapes=[pltpu.VMEM((tokens, 2 * gc), BF16)],
        compiler_params=_params("parallel", "parallel", "parallel"),
        name="fourier_lat2",
    )(y6, p2, m2, cs_stack, wf)


OUT_COL_CHUNK = 1024


def _outproj_kernel(f_ref, g_ref, x_ref, mod_ref, fnw_ref, w_ref, o_ref):
    tm, fw = f_ref.shape
    d = w_ref.shape[-1]
    f = f_ref[...]
    g = g_ref[...]
    ss = jnp.zeros((tm, 1), F32)
    for c0 in range(0, d, OUT_COL_CHUNK):
        cs = slice(c0, c0 + OUT_COL_CHUNK)
        acc = _dot(f, w_ref[:fw, cs]) + _dot(g, w_ref[fw:, cs])
        pre = x_ref[:, cs] + mod_ref[2:3, cs] * acc
        ss = ss + jnp.sum(pre * pre, axis=-1, keepdims=True)
        o_ref[:, cs] = pre
    scale = lax.rsqrt(ss * (1.0 / d) + EPS)
    for c0 in range(0, d, OUT_COL_CHUNK):
        cs = slice(c0, c0 + OUT_COL_CHUNK)
        o_ref[:, cs] = (o_ref[:, cs] * scale) * fnw_ref[:, cs]


def _outproj(f2, g2, x2, mod3, fnw, w_bf, mod_row, name, tm=256):
    t, d = x2.shape
    fw = f2.shape[-1]
    return pl.pallas_call(
        _outproj_kernel,
        grid=(t // tm,),
        in_specs=[
            pl.BlockSpec((tm, fw), lambda i: (i, 0)),
            pl.BlockSpec((tm, d - fw), lambda i: (i, 0)),
            pl.BlockSpec((tm, d), lambda i: (i, 0)),
            pl.BlockSpec((None, 3, d), lambda i: (mod_row(i), 0, 0)),
            pl.BlockSpec((1, d), lambda i: (0, 0)),
            pl.BlockSpec((d, d), lambda i: (0, 0), pipeline_mode=pl.Buffered(1)),
        ],
        out_specs=pl.BlockSpec((tm, d), lambda i: (i, 0)),
        out_shape=jax.ShapeDtypeStruct((t, d), F32),
        compiler_params=_params("parallel"),
        name=name,
    )(f2, g2, x2, mod3, fnw, w_bf)


def kernel(x_prompt, x_sample, state_gla_fwd, state_gla_bwd, c, c_ctx, ada_w, ada_b, norm_w,
           w_in, w_alpha, b_alpha, w_fourier, gla_norm_w, w_out, final_norm_w):
    bp, lp, d = x_prompt.shape
    bs, ls, _ = x_sample.shape
    depth = ada_w.shape[0]
    assert depth == 1, "single-layer step"
    fw = d // 2
    gc = fw // F_GROUPS
    dk_total = w_alpha.shape[-1]
    n_main = 2 * fw + 2 * dk_total + 2 * (d - fw)
    gla_offsets = (2 * fw, 2 * fw + dk_total, 2 * fw + 2 * dk_total, 2 * fw + 2 * dk_total + (d - fw))
    rows = ls // GRID_W
    assert bs + 1 <= MOD_ROWS and rows == GRID_W and 8 * GLA_LOWRANK == LANES

    cc = jnp.zeros((MOD_ROWS, d), F32).at[:bs].set(c).at[bs].set(c_ctx)
    mod3 = _modulation(cc, ada_w[0], ada_b).reshape(MOD_ROWS, 3, d)

    w_all = w_in[0].astype(BF16)
    w_a = jnp.tile(w_in[0, :, n_main:], (1, LANES // (2 * GLA_LOWRANK))).astype(BF16)
    w_out_bf = w_out[0].astype(BF16)
    wf_bf = w_fourier[0].astype(BF16)
    nw = norm_w[0][None, :]
    fnw = final_norm_w[None, :]
    lr = GLA_LOWRANK
    wdir = jnp.zeros((2, 2 * lr, dk_total), F32).at[0, :lr].set(w_alpha[0, 0]).at[1, lr:].set(w_alpha[0, 1])
    wdir_hi = wdir.astype(BF16)
    wdir_lo = (wdir - wdir_hi.astype(F32)).astype(BF16)
    wz = jnp.concatenate([wdir_hi, wdir_hi, wdir_lo, jnp.zeros_like(wdir_hi)], axis=1)
    bal = b_alpha[0][:, None, :]
    gnw = gla_norm_w[0][:, None, :]
    pos_in_block = np.arange(GLA_BLOCK)
    same_chunk = (pos_in_block[:, None] // GLA_CHUNK) == (pos_in_block[None, :] // GLA_CHUNK)
    lower = pos_in_block[:, None] >= pos_in_block[None, :]
    tri = jnp.asarray(np.stack([same_chunk & lower, same_chunk & lower.T]).astype(np.float32)).astype(BF16)

    cc_c, cc_s = _dft_mats(gc)
    cs = jnp.asarray(np.concatenate([cc_c, cc_s], axis=1)).astype(BF16)
    cs_stack = jnp.asarray(np.concatenate([cc_c, cc_s], axis=0)).astype(BF16)
    cl_c, cl_s = _dft_mats(lp)
    cl = jnp.asarray(cl_c).astype(BF16)
    sl = jnp.asarray(cl_s).astype(BF16)
    g_c, g_s = _dft_mats(GRID_W)
    eye2 = np.eye(2, dtype=np.float32)
    a2, b2 = np.kron(eye2, g_c), np.kron(eye2, g_s)
    m2 = jnp.asarray(np.block([[a2, b2], [-b2, a2]])).astype(BF16)
    row_real, row_cplx = _row_dft_mats(GRID_W, COL_TILE)
    mr, mc = jnp.asarray(row_real).astype(BF16), jnp.asarray(row_cplx).astype(BF16)

    xp2 = x_prompt.reshape(bp * lp, d)
    ctx_row = lambda i: bs
    p_ctx, a_ctx = _inproj(xp2, mod3, nw, w_all, w_a, n_main, ctx_row, "inproj_ctx")
    p3 = p_ctx.reshape(bp, lp, n_main)
    g_ctx, new_f, new_b = _gla(p3, a_ctx.reshape(bp, lp, LANES), wz, bal, gnw, tri,
                               None, None, True, gla_offsets, "gla_ctx")
    f_ctx = _fourier_ctx(p3, cs, cl, sl, wf_bf)
    y_prompt = _outproj(f_ctx.reshape(bp * lp, fw), g_ctx.reshape(bp * lp, d - fw),
                        xp2, mod3, fnw, w_out_bf, ctx_row, "outproj_ctx")

    xs2 = x_sample.reshape(bs * ls, d)
    tm_i = 1024
    p_lat, a_lat = _inproj(xs2, mod3, nw, w_all, w_a, n_main,
                           lambda i: i // (ls // tm_i), "inproj_lat", tm=tm_i)
    (g_lat,) = _gla(p_lat.reshape(bs, ls, n_main), a_lat.reshape(bs, ls, LANES), wz, bal, gnw, tri,
                    state_gla_fwd, state_gla_bwd, False, gla_offsets, "gla_lat")
    y6 = _fourier_lat1(p_lat.reshape(bs, rows, GRID_W, n_main), mr, mc, gc)
    f_lat = _fourier_lat2(y6, p_lat, m2, cs_stack, wf_bf)
    tm_o = 256
    y_sample = _outproj(f_lat, g_lat.reshape(bs * ls, d - fw), xs2, mod3, fnw,
                        w_out_bf, lambda i: i // (ls // tm_o), "outproj_lat", tm=tm_o)

    return (y_prompt.reshape(bp, lp, d), y_sample.reshape(bs, ls, d), new_f, new_b)
```

```python
import functools

import numpy as np
import jax
import jax.numpy as jnp
from jax import lax
from jax.experimental import pallas as pl
from jax.experimental.pallas import tpu as pltpu

F32 = jnp.float32
BF16 = jnp.bfloat16

GRID_W = 64
F_GROUPS = 4
GLA_HEADS = 4
GLA_LOWRANK = 16
GLA_TAU = 16.0
GLA_CHUNK = 64
GLA_BLOCK = 256
GLA_SLOTS = 4
EPS = 1e-6

VMEM_LIMIT_BYTES = 58 * 1024 * 1024
LANES = 128
MOD_ROWS = 16


def _params(*sem):
    return pltpu.CompilerParams(dimension_semantics=sem, vmem_limit_bytes=VMEM_LIMIT_BYTES)


def _dot(a, b):
    return jnp.dot(a, b, preferred_element_type=F32)


def _split2(x):
    hi = x.astype(BF16)
    lo = (x - hi.astype(F32)).astype(BF16)
    return hi, lo


def _dft_mats(n):
    k = np.arange(n)
    ang = 2.0 * np.pi * ((k[:, None] * k[None, :]) % n) / n
    s = 1.0 / np.sqrt(n)
    return (np.cos(ang) * s).astype(np.float32), (np.sin(ang) * s).astype(np.float32)


def _mod_kernel(c_ref, w_ref, b_ref, o_ref):
    c = c_ref[...]
    s = c * jax.nn.sigmoid(c)
    s_hi, s_lo = _split2(s)
    w_hi, w_lo = _split2(w_ref[...])
    o_ref[...] = _dot(s_hi, w_hi) + _dot(s_hi, w_lo) + _dot(s_lo, w_hi) + b_ref[...]


def _modulation(cc, ada_w, ada_b, tn=512):
    d, n = ada_w.shape
    return pl.pallas_call(
        _mod_kernel,
        grid=(n // tn,),
        in_specs=[
            pl.BlockSpec((MOD_ROWS, d), lambda j: (0, 0)),
            pl.BlockSpec((d, tn), lambda j: (0, j)),
            pl.BlockSpec((1, tn), lambda j: (0, j)),
        ],
        out_specs=pl.BlockSpec((MOD_ROWS, tn), lambda j: (0, j)),
        out_shape=jax.ShapeDtypeStruct((MOD_ROWS, n), F32),
        compiler_params=_params("parallel"),
        name="mod",
    )(cc, ada_w, ada_b)


def _inproj_kernel(x_ref, mod_ref, nw_ref, w_ref, wa_ref, p_ref, a_ref, h0_ref, h1_ref, *, n_slices):
    i = pl.program_id(0)
    j = pl.program_id(1)
    rows = x_ref.shape[0]
    odd = i % 2

    def step(h_cur, h_prev, first):
        if not first:
            p_ref[...] = _dot(h_prev[...], w_ref[...]).astype(BF16)
        x = x_ref[...]
        ms = jnp.mean(x * x, axis=-1, keepdims=True)
        y = (x * lax.rsqrt(ms + EPS)) * nw_ref[...]
        h = y * (1.0 + mod_ref[1:2, :]) + mod_ref[0:1, :]
        start = pl.multiple_of(jnp.minimum(j, n_slices - 1) * rows, rows)
        h_cur[pl.ds(start, rows), :] = h.astype(BF16)
        if first:
            return

        @pl.when(j == 0)
        def _():
            a_ref[...] = _dot(h_prev[...], wa_ref[...])

    @pl.when(i == 0)
    def _():
        step(h0_ref, h1_ref, True)

    @pl.when((i > 0) & (odd == 0))
    def _():
        step(h0_ref, h1_ref, False)

    @pl.when(odd == 1)
    def _():
        step(h1_ref, h0_ref, False)


def _inproj(x2, mod3, nw, w_bf, wa_bf, n_main, mod_row, name, tm=1024, tn=512, n_slices=16):
    t, d = x2.shape
    n_i, n_j = t // tm, n_main // tn
    rows = tm // n_slices
    assert n_slices <= n_j and rows % 8 == 0
    tile = lambda i: jnp.minimum(i, n_i - 1)
    done = lambda i: jnp.maximum(i - 1, 0)
    col = lambda i, j: jnp.where(i == 0, 0, j)
    return pl.pallas_call(
        functools.partial(_inproj_kernel, n_slices=n_slices),
        grid=(n_i + 1, n_j),
        in_specs=[
            pl.BlockSpec((rows, d), lambda i, j: (tile(i) * n_slices + jnp.minimum(j, n_slices - 1), 0)),
            pl.BlockSpec((None, 3, d), lambda i, j: (mod_row(tile(i)), 0, 0)),
            pl.BlockSpec((1, d), lambda i, j: (0, 0)),
            pl.BlockSpec((d, tn), lambda i, j: (0, col(i, j))),
            pl.BlockSpec((d, LANES), lambda i, j: (0, 0)),
        ],
        out_specs=[
            pl.BlockSpec((tm, tn), lambda i, j: (done(i), col(i, j))),
            pl.BlockSpec((tm, LANES), lambda i, j: (done(i), 0)),
        ],
        out_shape=[
            jax.ShapeDtypeStruct((t, n_main), BF16),
            jax.ShapeDtypeStruct((t, LANES), F32),
        ],
        scratch_shapes=[pltpu.VMEM((tm, d), BF16), pltpu.VMEM((tm, d), BF16)],
        compiler_params=_params("arbitrary", "arbitrary"),
        name=name,
    )(x2, mod3, nw, w_bf, wa_bf)


def _gla_kernel(*refs, seq_len, heads, has_state, emit_state):
    q_ref, k_ref, v_ref, gg_ref, a_ref, wz_ref, bal_ref, gnw_ref, tri_ref = refs[:9]
    pos = 9
    if has_state:
        s0f_ref, s0b_ref = refs[pos:pos + 2]
        pos += 2
    o_ref = refs[pos]
    pos += 1
    if emit_state:
        sf_ref, sb_ref = refs[pos:pos + 2]
        pos += 2
    of_ref, s_ref, b_scr, dcol_scr, qh_scr, kh_scr = refs[pos:pos + 6]

    c = GLA_CHUNK
    n_sub = GLA_BLOCK // c
    n_blocks = seq_len // GLA_BLOCK
    dk = q_ref.shape[-1] // heads
    dv = v_ref.shape[-1] // heads
    kl = lambda hd: slice(hd * dk, (hd + 1) * dk)
    vl = lambda hd: slice(hd * dv, (hd + 1) * dv)
    q_scale = float(dk) ** -0.5
    lrow = lax.broadcasted_iota(jnp.int32, (c, GLA_BLOCK), 0)
    bcol = lax.broadcasted_iota(jnp.int32, (c, GLA_BLOCK), 1)
    ccol = bcol // c
    lcol = bcol - ccol * c
    a_group = lax.broadcasted_iota(jnp.int32, (GLA_BLOCK, LANES), 1) // (2 * GLA_LOWRANK)
    contract_last = (((1,), (1,)), ((), ()))
    contract_first = (((0,), (0,)), ((), ()))
    sub = lambda x, j: x[j * c:(j + 1) * c]

    def rows_of(blk):
        start = blk * GLA_BLOCK
        if not isinstance(blk, int):
            start = pl.multiple_of(start, GLA_BLOCK)
        return pl.ds(start, GLA_BLOCK)

    def slot_of(blk, direction, hd):
        return (2 * hd + direction) * GLA_SLOTS + blk % GLA_SLOTS

    def chunk_masks(fwd):
        return [(ccol == j) & ((lrow >= lcol) if fwd else (lcol >= lrow)) for j in range(n_sub)]

    def stage_decay(blk, direction, hd, state_decays=True):
        a = a_ref[rows_of(blk), :]
        hi = a.astype(BF16).astype(F32)
        lhs = jnp.where(a_group == 1, a - hi, jnp.where(a_group == 3, 0.0, hi)).astype(BF16)
        z = _dot(lhs, wz_ref[direction, :, kl(hd)]) + bal_ref[direction, :, kl(hd)]
        g = (jnp.minimum(z, 0.0) - jnp.log(1.0 + jnp.exp(-jnp.abs(z)))) * (1.0 / GLA_TAU)
        tri = tri_ref[direction]
        g_hi, g_lo = _split2(g)
        b_scr[slot_of(blk, direction, hd)] = _dot(tri, g_hi) + _dot(tri, g_lo)
        if state_decays:
            dcol_scr[slot_of(blk, direction, hd)] = jnp.exp(jnp.sum(g.T, axis=-1, keepdims=True))

    def stage_intra(blk, direction, hd, first):
        fwd = direction == 0
        rs = rows_of(blk)
        scan = list(range(n_sub)) if fwd else list(range(n_sub - 1, -1, -1))
        at = {j: p for p, j in enumerate(scan)}
        masks0 = chunk_masks(fwd)
        b = b_scr[slot_of(blk, direction, hd)]
        t = [sub(b, j)[c - 1:c, :] if fwd else sub(b, j)[0:1, :] for j in range(n_sub)]
        b_end = jnp.concatenate([jnp.broadcast_to(t[j], (c, dk)) for j in range(n_sub)], axis=0)
        q = q_ref[rs, kl(hd)].astype(F32)
        k = k_ref[rs, kl(hd)].astype(F32)
        q32 = q * q_scale * jnp.exp(b)
        q_t = q32.astype(BF16)
        k_t = (k * jnp.exp(-b)).astype(BF16)
        kd32 = k * jnp.exp(b_end - b)
        k_dec = kd32.astype(BF16)

        before, acc = {}, jnp.zeros_like(t[0])
        for j in scan:
            before[j] = acc
            acc = acc + t[j]
        total = acc
        qh_scr[slot_of(blk, direction, hd)] = jnp.concatenate(
            [sub(q32, j) * jnp.exp(before[j]) for j in range(n_sub)], axis=0).astype(BF16)
        kh_scr[slot_of(blk, direction, hd)] = jnp.concatenate(
            [sub(kd32, j) * jnp.exp(total - before[j] - t[j]) for j in range(n_sub)], axis=0).astype(BF16)

        lhs, where = [q_t], {}
        n_rows = GLA_BLOCK
        for d in range(2, n_sub):
            for j in scan[d:]:
                between = scan[at[j] - d + 1:at[j]]
                gap = t[between[0]]
                for i in between[1:]:
                    gap = gap + t[i]
                lhs.append((sub(q32, j) * jnp.exp(gap)).astype(BF16))
                where[(j, d)] = n_rows
                n_rows += c
        r = lax.dot_general(jnp.concatenate(lhs, axis=0), k_dec, contract_last, preferred_element_type=F32)
        a0 = lax.dot_general(q_t, k_t, contract_last, preferred_element_type=F32)
        att_rows = []
        for j in range(n_sub):
            att = jnp.where(masks0[j], sub(a0, j), 0.0)
            for d in range(1, at[j] + 1):
                src = sub(r, j) if d == 1 else r[where[(j, d)]:where[(j, d)] + c]
                att = jnp.where(ccol == scan[at[j] - d], src, att)
            att_rows.append(att)
        att = jnp.concatenate(att_rows, axis=0).astype(BF16)
        o_intra = _dot(att, v_ref[rs, vl(hd)])
        if first:
            of_ref[rs, vl(hd)] = o_intra
        else:
            of_ref[rs, vl(hd)] += o_intra

    def stage_state(blk, direction, hd, first, from_zero=False):
        rs = rows_of(blk)
        upd = lax.dot_general(kh_scr[slot_of(blk, direction, hd)], v_ref[rs, vl(hd)], contract_first, preferred_element_type=F32)
        if from_zero:
            o_inter = jnp.zeros((GLA_BLOCK, dv), F32)
            s_ref[2 * hd + direction] = upd
        else:
            s = s_ref[2 * hd + direction]
            o_inter = _dot(qh_scr[slot_of(blk, direction, hd)], s.astype(BF16))
            s_ref[2 * hd + direction] = dcol_scr[slot_of(blk, direction, hd)] * s + upd
        if first:
            of_ref[rs, vl(hd)] += o_inter
        else:
            o = of_ref[rs, vl(hd)] + o_inter
            ms = jnp.mean(o * o, axis=-1, keepdims=True)
            y = (o * lax.rsqrt(ms + EPS)) * gnw_ref[hd]
            gg = gg_ref[rs, vl(hd)].astype(F32)
            o_ref[rs, vl(hd)] = (y * (gg * jax.nn.sigmoid(gg))).astype(BF16)

    n = n_blocks
    if has_state:
        for hd in range(heads):
            s_ref[2 * hd] = s0f_ref[hd]
            s_ref[2 * hd + 1] = s0b_ref[hd]
    else:
        s_ref[...] = jnp.zeros_like(s_ref)

    def both(stage, i, *args):
        for hd in range(heads):
            stage(i, 0, hd, *args)
            stage(n - 1 - i, 1, hd, *args)

    if n == 1:
        both(stage_decay, 0, has_state)
        for hd in range(heads):
            stage_intra(0, 0, hd, True)
            stage_intra(0, 1, hd, False)
        for hd in range(heads):
            stage_state(0, 0, hd, True, not has_state)
            stage_state(0, 1, hd, False, not has_state)
    else:
        half = n // 2
        both(stage_decay, 0)
        both(stage_decay, 1)
        both(stage_intra, 0, True)

        def first_half(i, carry):
            both(stage_state, i, True)
            both(stage_intra, i + 1, True)
            both(stage_decay, i + 2)
            return carry

        lax.fori_loop(0, half - 1, first_half, 0)
        both(stage_state, half - 1, True)
        both(stage_intra, half, False)
        both(stage_decay, half + 1)

        def second_half(i, carry):
            both(stage_state, i, False)
            both(stage_intra, i + 1, False)
            both(stage_decay, i + 2)
            return carry

        lax.fori_loop(half, n - 2, second_half, 0)
        both(stage_state, n - 2, False)
        both(stage_intra, n - 1, False)
        both(stage_state, n - 1, False)
    if emit_state:
        for hd in range(heads):
            sf_ref[hd] = s_ref[2 * hd]
            sb_ref[hd] = s_ref[2 * hd + 1]


def _gla(p3, a3, wz, bal, gnw, tri, s0f, s0b, emit_state, offsets, heads, name):
    bsz, seq_len, _ = p3.shape
    dk = wz.shape[-1] // GLA_HEADS
    dv = gnw.shape[-1]
    off_q, off_k, off_v, off_gg = offsets
    assert off_q % dk == 0 and off_k % dk == 0 and off_v % dv == 0 and off_gg % dv == 0
    n_blocks = seq_len // GLA_BLOCK
    assert seq_len % GLA_BLOCK == 0 and GLA_BLOCK % GLA_CHUNK == 0
    assert n_blocks == 1 or (n_blocks % 2 == 0 and n_blocks >= 4)
    wk, wv = heads * dk, heads * dv
    assert GLA_HEADS % heads == 0 and off_q % wk == 0 and off_k % wk == 0 and off_v % wv == 0 and off_gg % wv == 0
    qb, kb, vb, ggb = off_q // wk, off_k // wk, off_v // wv, off_gg // wv
    has_state = s0f is not None
    in_specs = [
        pl.BlockSpec((None, seq_len, wk), lambda b, h: (b, 0, qb + h)),
        pl.BlockSpec((None, seq_len, wk), lambda b, h: (b, 0, kb + h)),
        pl.BlockSpec((None, seq_len, wv), lambda b, h: (b, 0, vb + h)),
        pl.BlockSpec((None, seq_len, wv), lambda b, h: (b, 0, ggb + h)),
        pl.BlockSpec((None, seq_len, LANES), lambda b, h: (b, 0, 0)),
        pl.BlockSpec((2, LANES, wk), lambda b, h: (0, 0, h)),
        pl.BlockSpec((2, 1, wk), lambda b, h: (0, 0, h)),
        pl.BlockSpec((heads, 1, dv), lambda b, h: (h, 0, 0)),
        pl.BlockSpec((2, GLA_BLOCK, GLA_BLOCK), lambda b, h: (0, 0, 0)),
    ]
    args = [p3, p3, p3, p3, a3, wz, bal, gnw, tri]
    state_spec = pl.BlockSpec((None, None, heads, dk, dv), lambda b, h: (b, 0, h, 0, 0))
    if has_state:
        in_specs += [state_spec, state_spec]
        args += [s0f, s0b]
    out_specs = [pl.BlockSpec((None, seq_len, wv), lambda b, h: (b, 0, h))]
    out_shape = [jax.ShapeDtypeStruct((bsz, seq_len, GLA_HEADS * dv), BF16)]
    if emit_state:
        out_specs += [state_spec, state_spec]
        out_shape += [jax.ShapeDtypeStruct((bsz, 1, GLA_HEADS, dk, dv), F32)] * 2
    return pl.pallas_call(
        functools.partial(_gla_kernel, seq_len=seq_len, heads=heads, has_state=has_state, emit_state=emit_state),
        grid=(bsz, GLA_HEADS // heads),
        in_specs=in_specs,
        out_specs=out_specs,
        out_shape=out_shape,
        scratch_shapes=[
            pltpu.VMEM((seq_len, wv), F32),
            pltpu.VMEM((2 * heads, dk, dv), F32),
            pltpu.VMEM((2 * heads * GLA_SLOTS, GLA_BLOCK, dk), F32),
            pltpu.VMEM((2 * heads * GLA_SLOTS, dk, 1), F32),
            pltpu.VMEM((2 * heads * GLA_SLOTS, GLA_BLOCK, dk), BF16),
            pltpu.VMEM((2 * heads * GLA_SLOTS, GLA_BLOCK, dk), BF16),
        ],
        compiler_params=_params("parallel", "parallel"),
        name=name,
    )(*args)


def _fourier_ctx_kernel(u_ref, fg_ref, cs_ref, cl_ref, sl_ref, wf_ref, o_ref, re_ref):
    nb, seq_len, gc = u_ref.shape
    res = _dot(u_ref[...].reshape(nb * seq_len, gc), cs_ref[...])
    for i in range(nb):
        rs = slice(i * seq_len, (i + 1) * seq_len)
        uc = res[rs, :gc].astype(BF16)
        us = res[rs, gc:].astype(BF16)
        re_ref[rs, :] = (_dot(cl_ref[...], uc) - _dot(sl_ref[...], us)).astype(BF16)
    fg = fg_ref[...].reshape(nb * seq_len, gc).astype(F32)
    out = _dot(re_ref[...], wf_ref[...]) * (fg * jax.nn.sigmoid(fg))
    o_ref[...] = out.astype(BF16).reshape(nb, seq_len, gc)


def _fourier_ctx(p3, cs, cl, sl, wf, nb=4):
    bsz, seq_len, _ = p3.shape
    gc = wf.shape[-1]
    return pl.pallas_call(
        _fourier_ctx_kernel,
        grid=(bsz // nb, F_GROUPS),
        in_specs=[
            pl.BlockSpec((nb, seq_len, gc), lambda b, g: (b, 0, g)),
            pl.BlockSpec((nb, seq_len, gc), lambda b, g: (b, 0, F_GROUPS + g)),
            pl.BlockSpec((gc, 2 * gc), lambda b, g: (0, 0)),
            pl.BlockSpec((seq_len, seq_len), lambda b, g: (0, 0)),
            pl.BlockSpec((seq_len, seq_len), lambda b, g: (0, 0)),
            pl.BlockSpec((None, gc, gc), lambda b, g: (g, 0, 0)),
        ],
        out_specs=pl.BlockSpec((nb, seq_len, gc), lambda b, g: (b, 0, g)),
        out_shape=jax.ShapeDtypeStruct((bsz, seq_len, F_GROUPS * gc), BF16),
        scratch_shapes=[pltpu.VMEM((nb * seq_len, gc), BF16)],
        compiler_params=_params("parallel", "parallel"),
        name="fourier_ctx",
    )(p3, p3, cs, cl, sl, wf)


COL_TILE = 16
ROW_RADIX = 4


def _fourier_lat1_kernel(u_ref, mr_ref, mc_ref, y_ref):
    rows, cw, gc = u_ref.shape
    lo = rows // ROW_RADIX
    q = lo * cw
    u = [u_ref[lo * h:lo * (h + 1)].reshape(q, gc).astype(F32) for h in range(ROW_RADIX)]
    s02, s13, a, b = u[0] + u[2], u[1] + u[3], u[0] - u[2], u[1] - u[3]
    v0 = (s02 + s13).astype(BF16)
    v2 = (s02 - s13).astype(BF16)
    ab = jnp.concatenate([a.astype(BF16), b.astype(BF16)], axis=0)
    outs = {0: _dot(mr_ref[0], v0), 2: _dot(mr_ref[1], v2), 1: _dot(mc_ref[0], ab), 3: _dot(mc_ref[1], ab)}
    for k1, out in outs.items():
        o = out.astype(BF16)
        for part in range(2):
            blk = o[part * q:(part + 1) * q].reshape(lo, cw, gc)
            for k2 in range(lo):
                y_ref[part, k1 + ROW_RADIX * k2] = blk[k2]


def _row_dft_mats(n, tile):
    lo = n // ROW_RADIX
    eye = np.eye(tile)
    r_lo = np.arange(lo)
    mats = {}
    for k1 in range(ROW_RADIX):
        r_out = k1 + ROW_RADIX * np.arange(lo)
        ang = 2.0 * np.pi * ((r_out[:, None] * r_lo[None, :]) % n) / n
        mats[k1] = (np.kron(np.cos(ang) / np.sqrt(n), eye), np.kron(np.sin(ang) / np.sqrt(n), eye))
    real = np.stack([np.concatenate([mats[k][0], -mats[k][1]], axis=0) for k in (0, 2)])
    (c1, s1), (c3, s3) = mats[1], mats[3]
    cplx = np.stack([np.block([[c1, -s1], [-s1, -c1]]), np.block([[c3, s3], [-s3, c3]])])
    return real.astype(np.float32), cplx.astype(np.float32)


def _fourier_lat1(p4, mr, mc, gc):
    bsz, rows, cols, _ = p4.shape
    q = rows // ROW_RADIX * COL_TILE
    return pl.pallas_call(
        _fourier_lat1_kernel,
        grid=(bsz, cols // COL_TILE, F_GROUPS),
        in_specs=[
            pl.BlockSpec((None, rows, COL_TILE, gc), lambda b, cb, g: (b, 0, cb, g)),
            pl.BlockSpec((2, 2 * q, q), lambda b, cb, g: (0, 0, 0)),
            pl.BlockSpec((2, 2 * q, 2 * q), lambda b, cb, g: (0, 0, 0)),
        ],
        out_specs=pl.BlockSpec((None, None, 2, rows, COL_TILE, gc), lambda b, cb, g: (b, g, 0, 0, cb, 0)),
        out_shape=jax.ShapeDtypeStruct((bsz, F_GROUPS, 2, rows, cols, gc), BF16),
        compiler_params=_params("parallel", "parallel", "parallel"),
        name="fourier_lat1",
    )(p4, mr, mc)


ROW_TILE = 32


def _fourier_lat2_kernel(y_ref, fg_ref, m2_ref, cs_ref, wf_ref, o_ref, zz_ref):
    _, rt, cols, gc = y_ref.shape
    sub = 2 * cols
    for s in range(rt // 2):
        rs = slice(s * sub, (s + 1) * sub)
        st = jnp.concatenate([y_ref[0, 2 * s:2 * s + 2].reshape(sub, gc),
                              y_ref[1, 2 * s:2 * s + 2].reshape(sub, gc)], axis=0)
        out = _dot(m2_ref[...], st)
        zz_ref[rs, :gc] = out[:sub].astype(BF16)
        zz_ref[rs, gc:] = out[sub:].astype(BF16)
    re = _dot(zz_ref[...], cs_ref[...])
    fg = fg_ref[...].astype(F32)
    o_ref[...] = (_dot(re.astype(BF16), wf_ref[...]) * (fg * jax.nn.sigmoid(fg))).astype(BF16)


def _fourier_lat2(y6, p2, m2, cs_stack, wf):
    bsz, _, _, rows, cols, gc = y6.shape
    tokens = ROW_TILE * cols
    tiles = rows // ROW_TILE
    return pl.pallas_call(
        _fourier_lat2_kernel,
        grid=(bsz, tiles, F_GROUPS),
        in_specs=[
            pl.BlockSpec((None, None, 2, ROW_TILE, cols, gc), lambda b, rb, g: (b, g, 0, rb, 0, 0)),
            pl.BlockSpec((tokens, gc), lambda b, rb, g: (b * tiles + rb, F_GROUPS + g)),
            pl.BlockSpec((4 * cols, 4 * cols), lambda b, rb, g: (0, 0)),
            pl.BlockSpec((2 * gc, gc), lambda b, rb, g: (0, 0)),
            pl.BlockSpec((None, gc, gc), lambda b, rb, g: (g, 0, 0)),
        ],
        out_specs=pl.BlockSpec((tokens, gc), lambda b, rb, g: (b * tiles + rb, g)),
        out_shape=jax.ShapeDtypeStruct((bsz * rows * cols, F_GROUPS * gc), BF16),
        scratch_shapes=[pltpu.VMEM((tokens, 2 * gc), BF16)],
        compiler_params=_params("parallel", "parallel", "parallel"),
        name="fourier_lat2",
    )(y6, p2, m2, cs_stack, wf)


OUT_COL_CHUNK = 1024


def _outproj_kernel(f_ref, g_ref, x_ref, mod_ref, fnw_ref, w_ref, o_ref):
    tm, fw = f_ref.shape
    d = w_ref.shape[-1]
    f = f_ref[...]
    g = g_ref[...]
    ss = jnp.zeros((tm, 1), F32)
    for c0 in range(0, d, OUT_COL_CHUNK):
        cs = slice(c0, c0 + OUT_COL_CHUNK)
        acc = _dot(f, w_ref[:fw, cs]) + _dot(g, w_ref[fw:, cs])
        pre = x_ref[:, cs] + mod_ref[2:3, cs] * acc
        ss = ss + jnp.sum(pre * pre, axis=-1, keepdims=True)
        o_ref[:, cs] = pre
    scale = lax.rsqrt(ss * (1.0 / d) + EPS)
    for c0 in range(0, d, OUT_COL_CHUNK):
        cs = slice(c0, c0 + OUT_COL_CHUNK)
        o_ref[:, cs] = (o_ref[:, cs] * scale) * fnw_ref[:, cs]


def _outproj(f2, g2, x2, mod3, fnw, w_bf, mod_row, name, tm=256):
    t, d = x2.shape
    fw = f2.shape[-1]
    return pl.pallas_call(
        _outproj_kernel,
        grid=(t // tm,),
        in_specs=[
            pl.BlockSpec((tm, fw), lambda i: (i, 0)),
            pl.BlockSpec((tm, d - fw), lambda i: (i, 0)),
            pl.BlockSpec((tm, d), lambda i: (i, 0)),
            pl.BlockSpec((None, 3, d), lambda i: (mod_row(i), 0, 0)),
            pl.BlockSpec((1, d), lambda i: (0, 0)),
            pl.BlockSpec((d, d), lambda i: (0, 0), pipeline_mode=pl.Buffered(1)),
        ],
        out_specs=pl.BlockSpec((tm, d), lambda i: (i, 0)),
        out_shape=jax.ShapeDtypeStruct((t, d), F32),
        compiler_params=_params("parallel"),
        name=name,
    )(f2, g2, x2, mod3, fnw, w_bf)


def kernel(x_prompt, x_sample, state_gla_fwd, state_gla_bwd, c, c_ctx, ada_w, ada_b, norm_w,
           w_in, w_alpha, b_alpha, w_fourier, gla_norm_w, w_out, final_norm_w):
    bp, lp, d = x_prompt.shape
    bs, ls, _ = x_sample.shape
    depth = ada_w.shape[0]
    assert depth == 1, "single-layer step"
    fw = d // 2
    gc = fw // F_GROUPS
    dk_total = w_alpha.shape[-1]
    n_main = 2 * fw + 2 * dk_total + 2 * (d - fw)
    gla_offsets = (2 * fw, 2 * fw + dk_total, 2 * fw + 2 * dk_total, 2 * fw + 2 * dk_total + (d - fw))
    rows = ls // GRID_W
    assert bs + 1 <= MOD_ROWS and rows == GRID_W and 8 * GLA_LOWRANK == LANES

    cc = jnp.zeros((MOD_ROWS, d), F32).at[:bs].set(c).at[bs].set(c_ctx)
    mod3 = _modulation(cc, ada_w[0], ada_b).reshape(MOD_ROWS, 3, d)

    w_all = w_in[0].astype(BF16)
    w_a = jnp.tile(w_in[0, :, n_main:], (1, LANES // (2 * GLA_LOWRANK))).astype(BF16)
    w_out_bf = w_out[0].astype(BF16)
    wf_bf = w_fourier[0].astype(BF16)
    nw = norm_w[0][None, :]
    fnw = final_norm_w[None, :]
    lr = GLA_LOWRANK
    wdir = jnp.zeros((2, 2 * lr, dk_total), F32).at[0, :lr].set(w_alpha[0, 0]).at[1, lr:].set(w_alpha[0, 1])
    wdir_hi = wdir.astype(BF16)
    wdir_lo = (wdir - wdir_hi.astype(F32)).astype(BF16)
    wz = jnp.concatenate([wdir_hi, wdir_hi, wdir_lo, jnp.zeros_like(wdir_hi)], axis=1)
    bal = b_alpha[0][:, None, :]
    gnw = gla_norm_w[0][:, None, :]
    pos_in_block = np.arange(GLA_BLOCK)
    same_chunk = (pos_in_block[:, None] // GLA_CHUNK) == (pos_in_block[None, :] // GLA_CHUNK)
    lower = pos_in_block[:, None] >= pos_in_block[None, :]
    tri = jnp.asarray(np.stack([same_chunk & lower, same_chunk & lower.T]).astype(np.float32)).astype(BF16)

    cc_c, cc_s = _dft_mats(gc)
    cs = jnp.asarray(np.concatenate([cc_c, cc_s], axis=1)).astype(BF16)
    cs_stack = jnp.asarray(np.concatenate([cc_c, cc_s], axis=0)).astype(BF16)
    cl_c, cl_s = _dft_mats(lp)
    cl = jnp.asarray(cl_c).astype(BF16)
    sl = jnp.asarray(cl_s).astype(BF16)
    g_c, g_s = _dft_mats(GRID_W)
    eye2 = np.eye(2, dtype=np.float32)
    a2, b2 = np.kron(eye2, g_c), np.kron(eye2, g_s)
    m2 = jnp.asarray(np.block([[a2, b2], [-b2, a2]])).astype(BF16)
    row_real, row_cplx = _row_dft_mats(GRID_W, COL_TILE)
    mr, mc = jnp.asarray(row_real).astype(BF16), jnp.asarray(row_cplx).astype(BF16)

    xp2 = x_prompt.reshape(bp * lp, d)
    ctx_row = lambda i: bs
    p_ctx, a_ctx = _inproj(xp2, mod3, nw, w_all, w_a, n_main, ctx_row, "inproj_ctx")
    p3 = p_ctx.reshape(bp, lp, n_main)
    g_ctx, new_f, new_b = _gla(p3, a_ctx.reshape(bp, lp, LANES), wz, bal, gnw, tri,
                               None, None, True, gla_offsets, GLA_HEADS, "gla_ctx")
    f_ctx = _fourier_ctx(p3, cs, cl, sl, wf_bf)
    y_prompt = _outproj(f_ctx.reshape(bp * lp, fw), g_ctx.reshape(bp * lp, d - fw),
                        xp2, mod3, fnw, w_out_bf, ctx_row, "outproj_ctx")

    xs2 = x_sample.reshape(bs * ls, d)
    tm_i = 1024
    p_lat, a_lat = _inproj(xs2, mod3, nw, w_all, w_a, n_main,
                           lambda i: i // (ls // tm_i), "inproj_lat", tm=tm_i)
    (g_lat,) = _gla(p_lat.reshape(bs, ls, n_main), a_lat.reshape(bs, ls, LANES), wz, bal, gnw, tri,
                    state_gla_fwd, state_gla_bwd, False, gla_offsets, 1, "gla_lat")
    y6 = _fourier_lat1(p_lat.reshape(bs, rows, GRID_W, n_main), mr, mc, gc)
    f_lat = _fourier_lat2(y6, p_lat, m2, cs_stack, wf_bf)
    tm_o = 256
    y_sample = _outproj(f_lat, g_lat.reshape(bs * ls, d - fw), xs2, mod3, fnw,
                        w_out_bf, lambda i: i // (ls // tm_o), "outproj_lat", tm=tm_o)

    return (y_prompt.reshape(bp, lp, d), y_sample.reshape(bs, ls, d), new_f, new_b)
```

```python
import functools

import numpy as np
import jax
import jax.numpy as jnp
from jax import lax
from jax.experimental import pallas as pl
from jax.experimental.pallas import tpu as pltpu

F32 = jnp.float32
BF16 = jnp.bfloat16

GRID_W = 64
F_GROUPS = 4
GLA_HEADS = 4
GLA_LOWRANK = 16
GLA_TAU = 16.0
GLA_CHUNK = 64
GLA_BLOCK = 256
GLA_SLOTS = 4
EPS = 1e-6

VMEM_LIMIT_BYTES = 58 * 1024 * 1024
LANES = 128
MOD_ROWS = 16


def _params(*sem):
    return pltpu.CompilerParams(dimension_semantics=sem, vmem_limit_bytes=VMEM_LIMIT_BYTES)


def _dot(a, b):
    return jnp.dot(a, b, preferred_element_type=F32)


def _split2(x):
    hi = x.astype(BF16)
    lo = (x - hi.astype(F32)).astype(BF16)
    return hi, lo


def _dft_mats(n):
    k = np.arange(n)
    ang = 2.0 * np.pi * ((k[:, None] * k[None, :]) % n) / n
    s = 1.0 / np.sqrt(n)
    return (np.cos(ang) * s).astype(np.float32), (np.sin(ang) * s).astype(np.float32)


def _mod_kernel(c_ref, w_ref, b_ref, o_ref):
    c = c_ref[...]
    s = c * jax.nn.sigmoid(c)
    s_hi, s_lo = _split2(s)
    w_hi, w_lo = _split2(w_ref[...])
    o_ref[...] = _dot(s_hi, w_hi) + _dot(s_hi, w_lo) + _dot(s_lo, w_hi) + b_ref[...]


def _modulation(cc, ada_w, ada_b, tn=512):
    d, n = ada_w.shape
    return pl.pallas_call(
        _mod_kernel,
        grid=(n // tn,),
        in_specs=[
            pl.BlockSpec((MOD_ROWS, d), lambda j: (0, 0)),
            pl.BlockSpec((d, tn), lambda j: (0, j)),
            pl.BlockSpec((1, tn), lambda j: (0, j)),
        ],
        out_specs=pl.BlockSpec((MOD_ROWS, tn), lambda j: (0, j)),
        out_shape=jax.ShapeDtypeStruct((MOD_ROWS, n), F32),
        compiler_params=_params("parallel"),
        name="mod",
    )(cc, ada_w, ada_b)


def _inproj_kernel(x_ref, mod_ref, nw_ref, w_ref, wa_ref, p_ref, a_ref, h0_ref, h1_ref, *, n_slices):
    i = pl.program_id(0)
    j = pl.program_id(1)
    rows = x_ref.shape[0]
    odd = i % 2

    def step(h_cur, h_prev, first):
        if not first:
            p_ref[...] = _dot(h_prev[...], w_ref[...]).astype(BF16)
        x = x_ref[...]
        ms = jnp.mean(x * x, axis=-1, keepdims=True)
        y = (x * lax.rsqrt(ms + EPS)) * nw_ref[...]
        h = y * (1.0 + mod_ref[1:2, :]) + mod_ref[0:1, :]
        start = pl.multiple_of(jnp.minimum(j, n_slices - 1) * rows, rows)
        h_cur[pl.ds(start, rows), :] = h.astype(BF16)
        if first:
            return

        @pl.when(j == 0)
        def _():
            a_ref[...] = _dot(h_prev[...], wa_ref[...])

    @pl.when(i == 0)
    def _():
        step(h0_ref, h1_ref, True)

    @pl.when((i > 0) & (odd == 0))
    def _():
        step(h0_ref, h1_ref, False)

    @pl.when(odd == 1)
    def _():
        step(h1_ref, h0_ref, False)


def _inproj(x2, mod3, nw, w_bf, wa_bf, n_main, mod_row, name, tm=1024, tn=1024, n_slices=8):
    t, d = x2.shape
    n_i, n_j = t // tm, n_main // tn
    rows = tm // n_slices
    assert n_slices <= n_j and rows % 8 == 0
    tile = lambda i: jnp.minimum(i, n_i - 1)
    done = lambda i: jnp.maximum(i - 1, 0)
    col = lambda i, j: jnp.where(i == 0, 0, j)
    return pl.pallas_call(
        functools.partial(_inproj_kernel, n_slices=n_slices),
        grid=(n_i + 1, n_j),
        in_specs=[
            pl.BlockSpec((rows, d), lambda i, j: (tile(i) * n_slices + jnp.minimum(j, n_slices - 1), 0)),
            pl.BlockSpec((None, 3, d), lambda i, j: (mod_row(tile(i)), 0, 0)),
            pl.BlockSpec((1, d), lambda i, j: (0, 0)),
            pl.BlockSpec((d, tn), lambda i, j: (0, col(i, j))),
            pl.BlockSpec((d, LANES), lambda i, j: (0, 0)),
        ],
        out_specs=[
            pl.BlockSpec((tm, tn), lambda i, j: (done(i), col(i, j))),
            pl.BlockSpec((tm, LANES), lambda i, j: (done(i), 0)),
        ],
        out_shape=[
            jax.ShapeDtypeStruct((t, n_main), BF16),
            jax.ShapeDtypeStruct((t, LANES), F32),
        ],
        scratch_shapes=[pltpu.VMEM((tm, d), BF16), pltpu.VMEM((tm, d), BF16)],
        compiler_params=_params("arbitrary", "arbitrary"),
        name=name,
    )(x2, mod3, nw, w_bf, wa_bf)


def _gla_kernel(*refs, seq_len, heads, has_state, emit_state):
    q_ref, k_ref, v_ref, gg_ref, a_ref, wz_ref, bal_ref, gnw_ref, tri_ref = refs[:9]
    pos = 9
    if has_state:
        s0f_ref, s0b_ref = refs[pos:pos + 2]
        pos += 2
    o_ref = refs[pos]
    pos += 1
    if emit_state:
        sf_ref, sb_ref = refs[pos:pos + 2]
        pos += 2
    of_ref, s_ref, b_scr, dcol_scr, qh_scr, kh_scr = refs[pos:pos + 6]

    c = GLA_CHUNK
    n_sub = GLA_BLOCK // c
    n_blocks = seq_len // GLA_BLOCK
    dk = q_ref.shape[-1] // heads
    dv = v_ref.shape[-1] // heads
    kl = lambda hd: slice(hd * dk, (hd + 1) * dk)
    vl = lambda hd: slice(hd * dv, (hd + 1) * dv)
    q_scale = float(dk) ** -0.5
    lrow = lax.broadcasted_iota(jnp.int32, (c, GLA_BLOCK), 0)
    bcol = lax.broadcasted_iota(jnp.int32, (c, GLA_BLOCK), 1)
    ccol = bcol // c
    lcol = bcol - ccol * c
    a_group = lax.broadcasted_iota(jnp.int32, (GLA_BLOCK, LANES), 1) // (2 * GLA_LOWRANK)
    contract_last = (((1,), (1,)), ((), ()))
    contract_first = (((0,), (0,)), ((), ()))
    sub = lambda x, j: x[j * c:(j + 1) * c]

    def rows_of(blk):
        start = blk * GLA_BLOCK
        if not isinstance(blk, int):
            start = pl.multiple_of(start, GLA_BLOCK)
        return pl.ds(start, GLA_BLOCK)

    def slot_of(blk, direction, hd):
        return (2 * hd + direction) * GLA_SLOTS + blk % GLA_SLOTS

    def chunk_masks(fwd):
        return [(ccol == j) & ((lrow >= lcol) if fwd else (lcol >= lrow)) for j in range(n_sub)]

    def stage_decay(blk, direction, hd, state_decays=True):
        a = a_ref[rows_of(blk), :]
        hi = a.astype(BF16).astype(F32)
        lhs = jnp.where(a_group == 1, a - hi, jnp.where(a_group == 3, 0.0, hi)).astype(BF16)
        z = _dot(lhs, wz_ref[direction, :, kl(hd)]) + bal_ref[direction, :, kl(hd)]
        g = (jnp.minimum(z, 0.0) - jnp.log(1.0 + jnp.exp(-jnp.abs(z)))) * (1.0 / GLA_TAU)
        tri = tri_ref[direction]
        g_hi, g_lo = _split2(g)
        b_scr[slot_of(blk, direction, hd)] = _dot(tri, g_hi) + _dot(tri, g_lo)
        if state_decays:
            dcol_scr[slot_of(blk, direction, hd)] = jnp.exp(jnp.sum(g.T, axis=-1, keepdims=True))

    def stage_intra(blk, direction, hd, first):
        fwd = direction == 0
        rs = rows_of(blk)
        scan = list(range(n_sub)) if fwd else list(range(n_sub - 1, -1, -1))
        at = {j: p for p, j in enumerate(scan)}
        masks0 = chunk_masks(fwd)
        b = b_scr[slot_of(blk, direction, hd)]
        t = [sub(b, j)[c - 1:c, :] if fwd else sub(b, j)[0:1, :] for j in range(n_sub)]
        b_end = jnp.concatenate([jnp.broadcast_to(t[j], (c, dk)) for j in range(n_sub)], axis=0)
        q = q_ref[rs, kl(hd)].astype(F32)
        k = k_ref[rs, kl(hd)].astype(F32)
        q32 = q * q_scale * jnp.exp(b)
        q_t = q32.astype(BF16)
        k_t = (k * jnp.exp(-b)).astype(BF16)
        kd32 = k * jnp.exp(b_end - b)
        k_dec = kd32.astype(BF16)

        before, acc = {}, jnp.zeros_like(t[0])
        for j in scan:
            before[j] = acc
            acc = acc + t[j]
        total = acc
        qh_scr[slot_of(blk, direction, hd)] = jnp.concatenate(
            [sub(q32, j) * jnp.exp(before[j]) for j in range(n_sub)], axis=0).astype(BF16)
        kh_scr[slot_of(blk, direction, hd)] = jnp.concatenate(
            [sub(kd32, j) * jnp.exp(total - before[j] - t[j]) for j in range(n_sub)], axis=0).astype(BF16)

        lhs, where = [q_t], {}
        n_rows = GLA_BLOCK
        for d in range(2, n_sub):
            for j in scan[d:]:
                between = scan[at[j] - d + 1:at[j]]
                gap = t[between[0]]
                for i in between[1:]:
                    gap = gap + t[i]
                lhs.append((sub(q32, j) * jnp.exp(gap)).astype(BF16))
                where[(j, d)] = n_rows
                n_rows += c
        r = lax.dot_general(jnp.concatenate(lhs, axis=0), k_dec, contract_last, preferred_element_type=F32)
        a0 = lax.dot_general(q_t, k_t, contract_last, preferred_element_type=F32)
        att_rows = []
        for j in range(n_sub):
            att = jnp.where(masks0[j], sub(a0, j), 0.0)
            for d in range(1, at[j] + 1):
                src = sub(r, j) if d == 1 else r[where[(j, d)]:where[(j, d)] + c]
                att = jnp.where(ccol == scan[at[j] - d], src, att)
            att_rows.append(att)
        att = jnp.concatenate(att_rows, axis=0).astype(BF16)
        o_intra = _dot(att, v_ref[rs, vl(hd)])
        if first:
            of_ref[rs, vl(hd)] = o_intra
        else:
            of_ref[rs, vl(hd)] += o_intra

    def stage_state(blk, direction, hd, first, from_zero=False):
        rs = rows_of(blk)
        upd = lax.dot_general(kh_scr[slot_of(blk, direction, hd)], v_ref[rs, vl(hd)], contract_first, preferred_element_type=F32)
        if from_zero:
            o_inter = jnp.zeros((GLA_BLOCK, dv), F32)
            s_ref[2 * hd + direction] = upd
        else:
            s = s_ref[2 * hd + direction]
            o_inter = _dot(qh_scr[slot_of(blk, direction, hd)], s.astype(BF16))
            s_ref[2 * hd + direction] = dcol_scr[slot_of(blk, direction, hd)] * s + upd
        if first:
            of_ref[rs, vl(hd)] += o_inter
        else:
            o = of_ref[rs, vl(hd)] + o_inter
            ms = jnp.mean(o * o, axis=-1, keepdims=True)
            y = (o * lax.rsqrt(ms + EPS)) * gnw_ref[hd]
            gg = gg_ref[rs, vl(hd)].astype(F32)
            o_ref[rs, vl(hd)] = (y * (gg * jax.nn.sigmoid(gg))).astype(BF16)

    n = n_blocks
    if has_state:
        for hd in range(heads):
            s_ref[2 * hd] = s0f_ref[hd]
            s_ref[2 * hd + 1] = s0b_ref[hd]
    else:
        s_ref[...] = jnp.zeros_like(s_ref)

    def both(stage, i, *args):
        for hd in range(heads):
            stage(i, 0, hd, *args)
            stage(n - 1 - i, 1, hd, *args)

    if n == 1:
        both(stage_decay, 0, has_state)
        for hd in range(heads):
            stage_intra(0, 0, hd, True)
            stage_intra(0, 1, hd, False)
        for hd in range(heads):
            stage_state(0, 0, hd, True, not has_state)
            stage_state(0, 1, hd, False, not has_state)
    else:
        half = n // 2
        both(stage_decay, 0)
        both(stage_decay, 1)
        both(stage_intra, 0, True)

        def first_half(i, carry):
            both(stage_state, i, True)
            both(stage_intra, i + 1, True)
            both(stage_decay, i + 2)
            return carry

        lax.fori_loop(0, half - 1, first_half, 0)
        both(stage_state, half - 1, True)
        both(stage_intra, half, False)
        both(stage_decay, half + 1)

        def second_half(i, carry):
            both(stage_state, i, False)
            both(stage_intra, i + 1, False)
            both(stage_decay, i + 2)
            return carry

        lax.fori_loop(half, n - 2, second_half, 0)
        both(stage_state, n - 2, False)
        both(stage_intra, n - 1, False)
        both(stage_state, n - 1, False)
    if emit_state:
        for hd in range(heads):
            sf_ref[hd] = s_ref[2 * hd]
            sb_ref[hd] = s_ref[2 * hd + 1]


def _gla(p3, a3, wz, bal, gnw, tri, s0f, s0b, emit_state, offsets, heads, name):
    bsz, seq_len, _ = p3.shape
    dk = wz.shape[-1] // GLA_HEADS
    dv = gnw.shape[-1]
    off_q, off_k, off_v, off_gg = offsets
    assert off_q % dk == 0 and off_k % dk == 0 and off_v % dv == 0 and off_gg % dv == 0
    n_blocks = seq_len // GLA_BLOCK
    assert seq_len % GLA_BLOCK == 0 and GLA_BLOCK % GLA_CHUNK == 0
    assert n_blocks == 1 or (n_blocks % 2 == 0 and n_blocks >= 4)
    wk, wv = heads * dk, heads * dv
    assert GLA_HEADS % heads == 0 and off_q % wk == 0 and off_k % wk == 0 and off_v % wv == 0 and off_gg % wv == 0
    qb, kb, vb, ggb = off_q // wk, off_k // wk, off_v // wv, off_gg // wv
    has_state = s0f is not None
    in_specs = [
        pl.BlockSpec((None, seq_len, wk), lambda b, h: (b, 0, qb + h)),
        pl.BlockSpec((None, seq_len, wk), lambda b, h: (b, 0, kb + h)),
        pl.BlockSpec((None, seq_len, wv), lambda b, h: (b, 0, vb + h)),
        pl.BlockSpec((None, seq_len, wv), lambda b, h: (b, 0, ggb + h)),
        pl.BlockSpec((None, seq_len, LANES), lambda b, h: (b, 0, 0)),
        pl.BlockSpec((2, LANES, wk), lambda b, h: (0, 0, h)),
        pl.BlockSpec((2, 1, wk), lambda b, h: (0, 0, h)),
        pl.BlockSpec((heads, 1, dv), lambda b, h: (h, 0, 0)),
        pl.BlockSpec((2, GLA_BLOCK, GLA_BLOCK), lambda b, h: (0, 0, 0)),
    ]
    args = [p3, p3, p3, p3, a3, wz, bal, gnw, tri]
    state_spec = pl.BlockSpec((None, None, heads, dk, dv), lambda b, h: (b, 0, h, 0, 0))
    if has_state:
        in_specs += [state_spec, state_spec]
        args += [s0f, s0b]
    out_specs = [pl.BlockSpec((None, seq_len, wv), lambda b, h: (b, 0, h))]
    out_shape = [jax.ShapeDtypeStruct((bsz, seq_len, GLA_HEADS * dv), BF16)]
    if emit_state:
        out_specs += [state_spec, state_spec]
        out_shape += [jax.ShapeDtypeStruct((bsz, 1, GLA_HEADS, dk, dv), F32)] * 2
    return pl.pallas_call(
        functools.partial(_gla_kernel, seq_len=seq_len, heads=heads, has_state=has_state, emit_state=emit_state),
        grid=(bsz, GLA_HEADS // heads),
        in_specs=in_specs,
        out_specs=out_specs,
        out_shape=out_shape,
        scratch_shapes=[
            pltpu.VMEM((seq_len, wv), F32),
            pltpu.VMEM((2 * heads, dk, dv), F32),
            pltpu.VMEM((2 * heads * GLA_SLOTS, GLA_BLOCK, dk), F32),
            pltpu.VMEM((2 * heads * GLA_SLOTS, dk, 1), F32),
            pltpu.VMEM((2 * heads * GLA_SLOTS, GLA_BLOCK, dk), BF16),
            pltpu.VMEM((2 * heads * GLA_SLOTS, GLA_BLOCK, dk), BF16),
        ],
        compiler_params=_params("parallel", "parallel"),
        name=name,
    )(*args)


def _fourier_ctx_kernel(u_ref, fg_ref, cs_ref, cl_ref, sl_ref, wf_ref, o_ref, re_ref):
    nb, seq_len, gc = u_ref.shape
    res = _dot(u_ref[...].reshape(nb * seq_len, gc), cs_ref[...])
    for i in range(nb):
        rs = slice(i * seq_len, (i + 1) * seq_len)
        uc = res[rs, :gc].astype(BF16)
        us = res[rs, gc:].astype(BF16)
        re_ref[rs, :] = (_dot(cl_ref[...], uc) - _dot(sl_ref[...], us)).astype(BF16)
    fg = fg_ref[...].reshape(nb * seq_len, gc).astype(F32)
    out = _dot(re_ref[...], wf_ref[...]) * (fg * jax.nn.sigmoid(fg))
    o_ref[...] = out.astype(BF16).reshape(nb, seq_len, gc)


def _fourier_ctx(p3, cs, cl, sl, wf, nb=4):
    bsz, seq_len, _ = p3.shape
    gc = wf.shape[-1]
    return pl.pallas_call(
        _fourier_ctx_kernel,
        grid=(bsz // nb, F_GROUPS),
        in_specs=[
            pl.BlockSpec((nb, seq_len, gc), lambda b, g: (b, 0, g)),
            pl.BlockSpec((nb, seq_len, gc), lambda b, g: (b, 0, F_GROUPS + g)),
            pl.BlockSpec((gc, 2 * gc), lambda b, g: (0, 0)),
            pl.BlockSpec((seq_len, seq_len), lambda b, g: (0, 0)),
            pl.BlockSpec((seq_len, seq_len), lambda b, g: (0, 0)),
            pl.BlockSpec((None, gc, gc), lambda b, g: (g, 0, 0)),
        ],
        out_specs=pl.BlockSpec((nb, seq_len, gc), lambda b, g: (b, 0, g)),
        out_shape=jax.ShapeDtypeStruct((bsz, seq_len, F_GROUPS * gc), BF16),
        scratch_shapes=[pltpu.VMEM((nb * seq_len, gc), BF16)],
        compiler_params=_params("parallel", "parallel"),
        name="fourier_ctx",
    )(p3, p3, cs, cl, sl, wf)


COL_TILE = 16
ROW_RADIX = 4


def _fourier_lat1_kernel(u_ref, mr_ref, mc_ref, y_ref):
    rows, cw, gc = u_ref.shape
    lo = rows // ROW_RADIX
    q = lo * cw
    u = [u_ref[lo * h:lo * (h + 1)].reshape(q, gc).astype(F32) for h in range(ROW_RADIX)]
    s02, s13, a, b = u[0] + u[2], u[1] + u[3], u[0] - u[2], u[1] - u[3]
    v0 = (s02 + s13).astype(BF16)
    v2 = (s02 - s13).astype(BF16)
    ab = jnp.concatenate([a.astype(BF16), b.astype(BF16)], axis=0)
    outs = {0: _dot(mr_ref[0], v0), 2: _dot(mr_ref[1], v2), 1: _dot(mc_ref[0], ab), 3: _dot(mc_ref[1], ab)}
    for k1, out in outs.items():
        o = out.astype(BF16)
        for part in range(2):
            blk = o[part * q:(part + 1) * q].reshape(lo, cw, gc)
            for k2 in range(lo):
                y_ref[part, k1 + ROW_RADIX * k2] = blk[k2]


def _row_dft_mats(n, tile):
    lo = n // ROW_RADIX
    eye = np.eye(tile)
    r_lo = np.arange(lo)
    mats = {}
    for k1 in range(ROW_RADIX):
        r_out = k1 + ROW_RADIX * np.arange(lo)
        ang = 2.0 * np.pi * ((r_out[:, None] * r_lo[None, :]) % n) / n
        mats[k1] = (np.kron(np.cos(ang) / np.sqrt(n), eye), np.kron(np.sin(ang) / np.sqrt(n), eye))
    real = np.stack([np.concatenate([mats[k][0], -mats[k][1]], axis=0) for k in (0, 2)])
    (c1, s1), (c3, s3) = mats[1], mats[3]
    cplx = np.stack([np.block([[c1, -s1], [-s1, -c1]]), np.block([[c3, s3], [-s3, c3]])])
    return real.astype(np.float32), cplx.astype(np.float32)


def _fourier_lat1(p4, mr, mc, gc):
    bsz, rows, cols, _ = p4.shape
    q = rows // ROW_RADIX * COL_TILE
    return pl.pallas_call(
        _fourier_lat1_kernel,
        grid=(bsz, cols // COL_TILE, F_GROUPS),
        in_specs=[
            pl.BlockSpec((None, rows, COL_TILE, gc), lambda b, cb, g: (b, 0, cb, g)),
            pl.BlockSpec((2, 2 * q, q), lambda b, cb, g: (0, 0, 0)),
            pl.BlockSpec((2, 2 * q, 2 * q), lambda b, cb, g: (0, 0, 0)),
        ],
        out_specs=pl.BlockSpec((None, None, 2, rows, COL_TILE, gc), lambda b, cb, g: (b, g, 0, 0, cb, 0)),
        out_shape=jax.ShapeDtypeStruct((bsz, F_GROUPS, 2, rows, cols, gc), BF16),
        compiler_params=_params("parallel", "parallel", "parallel"),
        name="fourier_lat1",
    )(p4, mr, mc)


ROW_TILE = 32


def _fourier_lat2_kernel(y_ref, fg_ref, m2_ref, cs_ref, wf_ref, o_ref, zz_ref):
    _, rt, cols, gc = y_ref.shape
    sub = 2 * cols
    for s in range(rt // 2):
        rs = slice(s * sub, (s + 1) * sub)
        st = jnp.concatenate([y_ref[0, 2 * s:2 * s + 2].reshape(sub, gc),
                              y_ref[1, 2 * s:2 * s + 2].reshape(sub, gc)], axis=0)
        out = _dot(m2_ref[...], st)
        zz_ref[rs, :gc] = out[:sub].astype(BF16)
        zz_ref[rs, gc:] = out[sub:].astype(BF16)
    re = _dot(zz_ref[...], cs_ref[...])
    fg = fg_ref[...].astype(F32)
    o_ref[...] = (_dot(re.astype(BF16), wf_ref[...]) * (fg * jax.nn.sigmoid(fg))).astype(BF16)


def _fourier_lat2(y6, p2, m2, cs_stack, wf):
    bsz, _, _, rows, cols, gc = y6.shape
    tokens = ROW_TILE * cols
    tiles = rows // ROW_TILE
    return pl.pallas_call(
        _fourier_lat2_kernel,
        grid=(bsz, tiles, F_GROUPS),
        in_specs=[
            pl.BlockSpec((None, None, 2, ROW_TILE, cols, gc), lambda b, rb, g: (b, g, 0, rb, 0, 0)),
            pl.BlockSpec((tokens, gc), lambda b, rb, g: (b * tiles + rb, F_GROUPS + g)),
            pl.BlockSpec((4 * cols, 4 * cols), lambda b, rb, g: (0, 0)),
            pl.BlockSpec((2 * gc, gc), lambda b, rb, g: (0, 0)),
            pl.BlockSpec((None, gc, gc), lambda b, rb, g: (g, 0, 0)),
        ],
        out_specs=pl.BlockSpec((tokens, gc), lambda b, rb, g: (b * tiles + rb, g)),
        out_shape=jax.ShapeDtypeStruct((bsz * rows * cols, F_GROUPS * gc), BF16),
        scratch_shapes=[pltpu.VMEM((tokens, 2 * gc), BF16)],
        compiler_params=_params("parallel", "parallel", "parallel"),
        name="fourier_lat2",
    )(y6, p2, m2, cs_stack, wf)


OUT_COL_CHUNK = 1024


def _outproj_kernel(f_ref, g_ref, x_ref, mod_ref, fnw_ref, w_ref, o_ref):
    tm, fw = f_ref.shape
    d = w_ref.shape[-1]
    f = f_ref[...]
    g = g_ref[...]
    ss = jnp.zeros((tm, 1), F32)
    for c0 in range(0, d, OUT_COL_CHUNK):
        cs = slice(c0, c0 + OUT_COL_CHUNK)
        acc = _dot(f, w_ref[:fw, cs]) + _dot(g, w_ref[fw:, cs])
        pre = x_ref[:, cs] + mod_ref[2:3, cs] * acc
        ss = ss + jnp.sum(pre * pre, axis=-1, keepdims=True)
        o_ref[:, cs] = pre
    scale = lax.rsqrt(ss * (1.0 / d) + EPS)
    for c0 in range(0, d, OUT_COL_CHUNK):
        cs = slice(c0, c0 + OUT_COL_CHUNK)
        o_ref[:, cs] = (o_ref[:, cs] * scale) * fnw_ref[:, cs]


def _outproj(f2, g2, x2, mod3, fnw, w_bf, mod_row, name, tm=256):
    t, d = x2.shape
    fw = f2.shape[-1]
    return pl.pallas_call(
        _outproj_kernel,
        grid=(t // tm,),
        in_specs=[
            pl.BlockSpec((tm, fw), lambda i: (i, 0)),
            pl.BlockSpec((tm, d - fw), lambda i: (i, 0)),
            pl.BlockSpec((tm, d), lambda i: (i, 0)),
            pl.BlockSpec((None, 3, d), lambda i: (mod_row(i), 0, 0)),
            pl.BlockSpec((1, d), lambda i: (0, 0)),
            pl.BlockSpec((d, d), lambda i: (0, 0), pipeline_mode=pl.Buffered(1)),
        ],
        out_specs=pl.BlockSpec((tm, d), lambda i: (i, 0)),
        out_shape=jax.ShapeDtypeStruct((t, d), F32),
        compiler_params=_params("parallel"),
        name=name,
    )(f2, g2, x2, mod3, fnw, w_bf)


def kernel(x_prompt, x_sample, state_gla_fwd, state_gla_bwd, c, c_ctx, ada_w, ada_b, norm_w,
           w_in, w_alpha, b_alpha, w_fourier, gla_norm_w, w_out, final_norm_w):
    bp, lp, d = x_prompt.shape
    bs, ls, _ = x_sample.shape
    depth = ada_w.shape[0]
    assert depth == 1, "single-layer step"
    fw = d // 2
    gc = fw // F_GROUPS
    dk_total = w_alpha.shape[-1]
    n_main = 2 * fw + 2 * dk_total + 2 * (d - fw)
    gla_offsets = (2 * fw, 2 * fw + dk_total, 2 * fw + 2 * dk_total, 2 * fw + 2 * dk_total + (d - fw))
    rows = ls // GRID_W
    assert bs + 1 <= MOD_ROWS and rows == GRID_W and 8 * GLA_LOWRANK == LANES

    cc = jnp.zeros((MOD_ROWS, d), F32).at[:bs].set(c).at[bs].set(c_ctx)
    mod3 = _modulation(cc, ada_w[0], ada_b).reshape(MOD_ROWS, 3, d)

    w_all = w_in[0].astype(BF16)
    w_a = jnp.tile(w_in[0, :, n_main:], (1, LANES // (2 * GLA_LOWRANK))).astype(BF16)
    w_out_bf = w_out[0].astype(BF16)
    wf_bf = w_fourier[0].astype(BF16)
    nw = norm_w[0][None, :]
    fnw = final_norm_w[None, :]
    lr = GLA_LOWRANK
    wdir = jnp.zeros((2, 2 * lr, dk_total), F32).at[0, :lr].set(w_alpha[0, 0]).at[1, lr:].set(w_alpha[0, 1])
    wdir_hi = wdir.astype(BF16)
    wdir_lo = (wdir - wdir_hi.astype(F32)).astype(BF16)
    wz = jnp.concatenate([wdir_hi, wdir_hi, wdir_lo, jnp.zeros_like(wdir_hi)], axis=1)
    bal = b_alpha[0][:, None, :]
    gnw = gla_norm_w[0][:, None, :]
    pos_in_block = np.arange(GLA_BLOCK)
    same_chunk = (pos_in_block[:, None] // GLA_CHUNK) == (pos_in_block[None, :] // GLA_CHUNK)
    lower = pos_in_block[:, None] >= pos_in_block[None, :]
    tri = jnp.asarray(np.stack([same_chunk & lower, same_chunk & lower.T]).astype(np.float32)).astype(BF16)

    cc_c, cc_s = _dft_mats(gc)
    cs = jnp.asarray(np.concatenate([cc_c, cc_s], axis=1)).astype(BF16)
    cs_stack = jnp.asarray(np.concatenate([cc_c, cc_s], axis=0)).astype(BF16)
    cl_c, cl_s = _dft_mats(lp)
    cl = jnp.asarray(cl_c).astype(BF16)
    sl = jnp.asarray(cl_s).astype(BF16)
    g_c, g_s = _dft_mats(GRID_W)
    eye2 = np.eye(2, dtype=np.float32)
    a2, b2 = np.kron(eye2, g_c), np.kron(eye2, g_s)
    m2 = jnp.asarray(np.block([[a2, b2], [-b2, a2]])).astype(BF16)
    row_real, row_cplx = _row_dft_mats(GRID_W, COL_TILE)
    mr, mc = jnp.asarray(row_real).astype(BF16), jnp.asarray(row_cplx).astype(BF16)

    xp2 = x_prompt.reshape(bp * lp, d)
    ctx_row = lambda i: bs
    p_ctx, a_ctx = _inproj(xp2, mod3, nw, w_all, w_a, n_main, ctx_row, "inproj_ctx")
    p3 = p_ctx.reshape(bp, lp, n_main)
    g_ctx, new_f, new_b = _gla(p3, a_ctx.reshape(bp, lp, LANES), wz, bal, gnw, tri,
                               None, None, True, gla_offsets, GLA_HEADS, "gla_ctx")
    f_ctx = _fourier_ctx(p3, cs, cl, sl, wf_bf)
    y_prompt = _outproj(f_ctx.reshape(bp * lp, fw), g_ctx.reshape(bp * lp, d - fw),
                        xp2, mod3, fnw, w_out_bf, ctx_row, "outproj_ctx")

    xs2 = x_sample.reshape(bs * ls, d)
    tm_i = 1024
    p_lat, a_lat = _inproj(xs2, mod3, nw, w_all, w_a, n_main,
                           lambda i: i // (ls // tm_i), "inproj_lat", tm=tm_i)
    (g_lat,) = _gla(p_lat.reshape(bs, ls, n_main), a_lat.reshape(bs, ls, LANES), wz, bal, gnw, tri,
                    state_gla_fwd, state_gla_bwd, False, gla_offsets, 1, "gla_lat")
    y6 = _fourier_lat1(p_lat.reshape(bs, rows, GRID_W, n_main), mr, mc, gc)
    f_lat = _fourier_lat2(y6, p_lat, m2, cs_stack, wf_bf)
    tm_o = 256
    y_sample = _outproj(f_lat, g_lat.reshape(bs * ls, d - fw), xs2, mod3, fnw,
                        w_out_bf, lambda i: i // (ls // tm_o), "outproj_lat", tm=tm_o)

    return (y_prompt.reshape(bp, lp, d), y_sample.reshape(bs, ls, d), new_f, new_b)
```

```python
import functools

import numpy as np
import jax
import jax.numpy as jnp
from jax import lax
from jax.experimental import pallas as pl
from jax.experimental.pallas import tpu as pltpu

F32 = jnp.float32
BF16 = jnp.bfloat16

GRID_W = 64
F_GROUPS = 4
GLA_HEADS = 4
GLA_LOWRANK = 16
GLA_TAU = 16.0
GLA_CHUNK = 64
GLA_BLOCK = 256
GLA_SLOTS = 4
EPS = 1e-6

VMEM_LIMIT_BYTES = 58 * 1024 * 1024
LANES = 128
MOD_ROWS = 16


def _params(*sem):
    return pltpu.CompilerParams(dimension_semantics=sem, vmem_limit_bytes=VMEM_LIMIT_BYTES)


def _dot(a, b):
    return jnp.dot(a, b, preferred_element_type=F32)


def _split2(x):
    hi = x.astype(BF16)
    lo = (x - hi.astype(F32)).astype(BF16)
    return hi, lo


def _dft_mats(n):
    k = np.arange(n)
    ang = 2.0 * np.pi * ((k[:, None] * k[None, :]) % n) / n
    s = 1.0 / np.sqrt(n)
    return (np.cos(ang) * s).astype(np.float32), (np.sin(ang) * s).astype(np.float32)


def _mod_kernel(c_ref, w_ref, b_ref, o_ref):
    c = c_ref[...]
    s = c * jax.nn.sigmoid(c)
    s_hi, s_lo = _split2(s)
    w_hi, w_lo = _split2(w_ref[...])
    o_ref[...] = _dot(s_hi, w_hi) + _dot(s_hi, w_lo) + _dot(s_lo, w_hi) + b_ref[...]


def _modulation(cc, ada_w, ada_b, tn=512):
    d, n = ada_w.shape
    return pl.pallas_call(
        _mod_kernel,
        grid=(n // tn,),
        in_specs=[
            pl.BlockSpec((MOD_ROWS, d), lambda j: (0, 0)),
            pl.BlockSpec((d, tn), lambda j: (0, j)),
            pl.BlockSpec((1, tn), lambda j: (0, j)),
        ],
        out_specs=pl.BlockSpec((MOD_ROWS, tn), lambda j: (0, j)),
        out_shape=jax.ShapeDtypeStruct((MOD_ROWS, n), F32),
        compiler_params=_params("parallel"),
        name="mod",
    )(cc, ada_w, ada_b)


def _inproj_kernel(x_ref, mod_ref, nw_ref, w_ref, wa_ref, p_ref, a_ref, h0_ref, h1_ref, *, n_slices):
    i = pl.program_id(0)
    j = pl.program_id(1)
    rows = x_ref.shape[0]
    odd = i % 2

    def step(h_cur, h_prev, first):
        if not first:
            p_ref[...] = _dot(h_prev[...], w_ref[...]).astype(BF16)
        x = x_ref[...]
        ms = jnp.mean(x * x, axis=-1, keepdims=True)
        y = (x * lax.rsqrt(ms + EPS)) * nw_ref[...]
        h = y * (1.0 + mod_ref[1:2, :]) + mod_ref[0:1, :]
        start = pl.multiple_of(jnp.minimum(j, n_slices - 1) * rows, rows)
        h_cur[pl.ds(start, rows), :] = h.astype(BF16)
        if first:
            return

        @pl.when(j == 0)
        def _():
            a_ref[...] = _dot(h_prev[...], wa_ref[...])

    @pl.when(i == 0)
    def _():
        step(h0_ref, h1_ref, True)

    @pl.when((i > 0) & (odd == 0))
    def _():
        step(h0_ref, h1_ref, False)

    @pl.when(odd == 1)
    def _():
        step(h1_ref, h0_ref, False)


def _inproj(x2, mod3, nw, w_bf, wa_bf, n_main, mod_row, name, tm=1024, tn=1024, n_slices=8):
    t, d = x2.shape
    n_i, n_j = t // tm, n_main // tn
    rows = tm // n_slices
    assert n_slices <= n_j and rows % 8 == 0
    tile = lambda i: jnp.minimum(i, n_i - 1)
    done = lambda i: jnp.maximum(i - 1, 0)
    col = lambda i, j: jnp.where(i == 0, 0, j)
    return pl.pallas_call(
        functools.partial(_inproj_kernel, n_slices=n_slices),
        grid=(n_i + 1, n_j),
        in_specs=[
            pl.BlockSpec((rows, d), lambda i, j: (tile(i) * n_slices + jnp.minimum(j, n_slices - 1), 0)),
            pl.BlockSpec((None, 3, d), lambda i, j: (mod_row(tile(i)), 0, 0)),
            pl.BlockSpec((1, d), lambda i, j: (0, 0)),
            pl.BlockSpec((d, tn), lambda i, j: (0, col(i, j))),
            pl.BlockSpec((d, LANES), lambda i, j: (0, 0)),
        ],
        out_specs=[
            pl.BlockSpec((tm, tn), lambda i, j: (done(i), col(i, j))),
            pl.BlockSpec((tm, LANES), lambda i, j: (done(i), 0)),
        ],
        out_shape=[
            jax.ShapeDtypeStruct((t, n_main), BF16),
            jax.ShapeDtypeStruct((t, LANES), F32),
        ],
        scratch_shapes=[pltpu.VMEM((tm, d), BF16), pltpu.VMEM((tm, d), BF16)],
        compiler_params=_params("arbitrary", "arbitrary"),
        name=name,
    )(x2, mod3, nw, w_bf, wa_bf)


def _gla_kernel(*refs, seq_len, heads, has_state, emit_state):
    q_ref, k_ref, v_ref, gg_ref, a_ref, wz_ref, bal_ref, gnw_ref, tri_ref = refs[:9]
    pos = 9
    if has_state:
        s0f_ref, s0b_ref = refs[pos:pos + 2]
        pos += 2
    o_ref = refs[pos]
    pos += 1
    if emit_state:
        sf_ref, sb_ref = refs[pos:pos + 2]
        pos += 2
    of_ref, s_ref, b_scr, dcol_scr, qh_scr, kh_scr = refs[pos:pos + 6]

    c = GLA_CHUNK
    n_sub = GLA_BLOCK // c
    n_blocks = seq_len // GLA_BLOCK
    dk = q_ref.shape[-1] // heads
    dv = v_ref.shape[-1] // heads
    kl = lambda hd: slice(hd * dk, (hd + 1) * dk)
    vl = lambda hd: slice(hd * dv, (hd + 1) * dv)
    q_scale = float(dk) ** -0.5
    lrow = lax.broadcasted_iota(jnp.int32, (c, GLA_BLOCK), 0)
    bcol = lax.broadcasted_iota(jnp.int32, (c, GLA_BLOCK), 1)
    ccol = bcol // c
    lcol = bcol - ccol * c
    a_group = lax.broadcasted_iota(jnp.int32, (GLA_BLOCK, LANES), 1) // (2 * GLA_LOWRANK)
    contract_last = (((1,), (1,)), ((), ()))
    contract_first = (((0,), (0,)), ((), ()))
    sub = lambda x, j: x[j * c:(j + 1) * c]

    def rows_of(blk):
        start = blk * GLA_BLOCK
        if not isinstance(blk, int):
            start = pl.multiple_of(start, GLA_BLOCK)
        return pl.ds(start, GLA_BLOCK)

    def slot_of(blk, direction, hd):
        return (2 * hd + direction) * GLA_SLOTS + blk % GLA_SLOTS

    def chunk_masks(fwd):
        return [(ccol == j) & ((lrow >= lcol) if fwd else (lcol >= lrow)) for j in range(n_sub)]

    def stage_decay(blk, direction, hd, state_decays=True):
        a = a_ref[rows_of(blk), :]
        hi = a.astype(BF16).astype(F32)
        lhs = jnp.where(a_group == 1, a - hi, jnp.where(a_group == 3, 0.0, hi)).astype(BF16)
        z = _dot(lhs, wz_ref[direction, :, kl(hd)]) + bal_ref[direction, :, kl(hd)]
        g = (jnp.minimum(z, 0.0) - jnp.log(1.0 + jnp.exp(-jnp.abs(z)))) * (1.0 / GLA_TAU)
        tri = tri_ref[direction]
        g_hi, g_lo = _split2(g)
        b_scr[slot_of(blk, direction, hd)] = _dot(tri, g_hi) + _dot(tri, g_lo)
        if state_decays:
            dcol_scr[slot_of(blk, direction, hd)] = jnp.exp(jnp.sum(g.T, axis=-1, keepdims=True))

    def stage_intra(blk, direction, hd, first):
        fwd = direction == 0
        rs = rows_of(blk)
        scan = list(range(n_sub)) if fwd else list(range(n_sub - 1, -1, -1))
        at = {j: p for p, j in enumerate(scan)}
        masks0 = chunk_masks(fwd)
        b = b_scr[slot_of(blk, direction, hd)]
        t = [sub(b, j)[c - 1:c, :] if fwd else sub(b, j)[0:1, :] for j in range(n_sub)]
        b_end = jnp.concatenate([jnp.broadcast_to(t[j], (c, dk)) for j in range(n_sub)], axis=0)
        q = q_ref[rs, kl(hd)].astype(F32)
        k = k_ref[rs, kl(hd)].astype(F32)
        q32 = q * q_scale * jnp.exp(b)
        q_t = q32.astype(BF16)
        k_t = (k * jnp.exp(-b)).astype(BF16)
        kd32 = k * jnp.exp(b_end - b)
        k_dec = kd32.astype(BF16)

        before, acc = {}, jnp.zeros_like(t[0])
        for j in scan:
            before[j] = acc
            acc = acc + t[j]
        total = acc
        qh_scr[slot_of(blk, direction, hd)] = jnp.concatenate(
            [sub(q32, j) * jnp.exp(before[j]) for j in range(n_sub)], axis=0).astype(BF16)
        kh_scr[slot_of(blk, direction, hd)] = jnp.concatenate(
            [sub(kd32, j) * jnp.exp(total - before[j] - t[j]) for j in range(n_sub)], axis=0).astype(BF16)

        lhs, where = [q_t], {}
        n_rows = GLA_BLOCK
        for d in range(2, n_sub):
            for j in scan[d:]:
                between = scan[at[j] - d + 1:at[j]]
                gap = t[between[0]]
                for i in between[1:]:
                    gap = gap + t[i]
                lhs.append((sub(q32, j) * jnp.exp(gap)).astype(BF16))
                where[(j, d)] = n_rows
                n_rows += c
        r = lax.dot_general(jnp.concatenate(lhs, axis=0), k_dec, contract_last, preferred_element_type=F32)
        a0 = lax.dot_general(q_t, k_t, contract_last, preferred_element_type=F32)
        att_rows = []
        for j in range(n_sub):
            att = jnp.where(masks0[j], sub(a0, j), 0.0)
            for d in range(1, at[j] + 1):
                src = sub(r, j) if d == 1 else r[where[(j, d)]:where[(j, d)] + c]
                att = jnp.where(ccol == scan[at[j] - d], src, att)
            att_rows.append(att)
        att = jnp.concatenate(att_rows, axis=0).astype(BF16)
        o_intra = _dot(att, v_ref[rs, vl(hd)])
        if first:
            of_ref[rs, vl(hd)] = o_intra
        else:
            of_ref[rs, vl(hd)] += o_intra

    def stage_state(blk, direction, hd, first, from_zero=False):
        rs = rows_of(blk)
        upd = lax.dot_general(kh_scr[slot_of(blk, direction, hd)], v_ref[rs, vl(hd)], contract_first, preferred_element_type=F32)
        if from_zero:
            o_inter = jnp.zeros((GLA_BLOCK, dv), F32)
            s_ref[2 * hd + direction] = upd
        else:
            s = s_ref[2 * hd + direction]
            o_inter = _dot(qh_scr[slot_of(blk, direction, hd)], s.astype(BF16))
            s_ref[2 * hd + direction] = dcol_scr[slot_of(blk, direction, hd)] * s + upd
        if first:
            of_ref[rs, vl(hd)] += o_inter
        else:
            o = of_ref[rs, vl(hd)] + o_inter
            ms = jnp.mean(o * o, axis=-1, keepdims=True)
            y = (o * lax.rsqrt(ms + EPS)) * gnw_ref[hd]
            gg = gg_ref[rs, vl(hd)].astype(F32)
            o_ref[rs, vl(hd)] = (y * (gg * jax.nn.sigmoid(gg))).astype(BF16)

    n = n_blocks
    if has_state:
        for hd in range(heads):
            s_ref[2 * hd] = s0f_ref[hd]
            s_ref[2 * hd + 1] = s0b_ref[hd]
    else:
        s_ref[...] = jnp.zeros_like(s_ref)

    def both(stage, i, *args):
        for hd in range(heads):
            stage(i, 0, hd, *args)
            stage(n - 1 - i, 1, hd, *args)

    if n == 1:
        both(stage_decay, 0, has_state)
        for hd in range(heads):
            stage_intra(0, 0, hd, True)
            stage_intra(0, 1, hd, False)
        for hd in range(heads):
            stage_state(0, 0, hd, True, not has_state)
            stage_state(0, 1, hd, False, not has_state)
    else:
        half = n // 2
        both(stage_decay, 0)
        both(stage_decay, 1)
        both(stage_intra, 0, True)

        def first_half(i, carry):
            both(stage_state, i, True)
            both(stage_intra, i + 1, True)
            both(stage_decay, i + 2)
            return carry

        lax.fori_loop(0, half - 1, first_half, 0)
        both(stage_state, half - 1, True)
        both(stage_intra, half, False)
        both(stage_decay, half + 1)

        def second_half(i, carry):
            both(stage_state, i, False)
            both(stage_intra, i + 1, False)
            both(stage_decay, i + 2)
            return carry

        lax.fori_loop(half, n - 2, second_half, 0)
        both(stage_state, n - 2, False)
        both(stage_intra, n - 1, False)
        both(stage_state, n - 1, False)
    if emit_state:
        for hd in range(heads):
            sf_ref[hd] = s_ref[2 * hd]
            sb_ref[hd] = s_ref[2 * hd + 1]


def _gla(p3, a3, wz, bal, gnw, tri, s0f, s0b, emit_state, offsets, heads, name):
    bsz, seq_len, _ = p3.shape
    dk = wz.shape[-1] // GLA_HEADS
    dv = gnw.shape[-1]
    off_q, off_k, off_v, off_gg = offsets
    assert off_q % dk == 0 and off_k % dk == 0 and off_v % dv == 0 and off_gg % dv == 0
    n_blocks = seq_len // GLA_BLOCK
    assert seq_len % GLA_BLOCK == 0 and GLA_BLOCK % GLA_CHUNK == 0
    assert n_blocks == 1 or (n_blocks % 2 == 0 and n_blocks >= 4)
    wk, wv = heads * dk, heads * dv
    assert GLA_HEADS % heads == 0 and off_q % wk == 0 and off_k % wk == 0 and off_v % wv == 0 and off_gg % wv == 0
    qb, kb, vb, ggb = off_q // wk, off_k // wk, off_v // wv, off_gg // wv
    has_state = s0f is not None
    in_specs = [
        pl.BlockSpec((None, seq_len, wk), lambda b, h: (b, 0, qb + h)),
        pl.BlockSpec((None, seq_len, wk), lambda b, h: (b, 0, kb + h)),
        pl.BlockSpec((None, seq_len, wv), lambda b, h: (b, 0, vb + h)),
        pl.BlockSpec((None, seq_len, wv), lambda b, h: (b, 0, ggb + h)),
        pl.BlockSpec((None, seq_len, LANES), lambda b, h: (b, 0, 0)),
        pl.BlockSpec((2, LANES, wk), lambda b, h: (0, 0, h)),
        pl.BlockSpec((2, 1, wk), lambda b, h: (0, 0, h)),
        pl.BlockSpec((heads, 1, dv), lambda b, h: (h, 0, 0)),
        pl.BlockSpec((2, GLA_BLOCK, GLA_BLOCK), lambda b, h: (0, 0, 0)),
    ]
    args = [p3, p3, p3, p3, a3, wz, bal, gnw, tri]
    state_spec = pl.BlockSpec((None, None, heads, dk, dv), lambda b, h: (b, 0, h, 0, 0))
    if has_state:
        in_specs += [state_spec, state_spec]
        args += [s0f, s0b]
    out_specs = [pl.BlockSpec((None, seq_len, wv), lambda b, h: (b, 0, h))]
    out_shape = [jax.ShapeDtypeStruct((bsz, seq_len, GLA_HEADS * dv), BF16)]
    if emit_state:
        out_specs += [state_spec, state_spec]
        out_shape += [jax.ShapeDtypeStruct((bsz, 1, GLA_HEADS, dk, dv), F32)] * 2
    return pl.pallas_call(
        functools.partial(_gla_kernel, seq_len=seq_len, heads=heads, has_state=has_state, emit_state=emit_state),
        grid=(bsz, GLA_HEADS // heads),
        in_specs=in_specs,
        out_specs=out_specs,
        out_shape=out_shape,
        scratch_shapes=[
            pltpu.VMEM((seq_len, wv), F32),
            pltpu.VMEM((2 * heads, dk, dv), F32),
            pltpu.VMEM((2 * heads * GLA_SLOTS, GLA_BLOCK, dk), F32),
            pltpu.VMEM((2 * heads * GLA_SLOTS, dk, 1), F32),
            pltpu.VMEM((2 * heads * GLA_SLOTS, GLA_BLOCK, dk), BF16),
            pltpu.VMEM((2 * heads * GLA_SLOTS, GLA_BLOCK, dk), BF16),
        ],
        compiler_params=_params("parallel", "parallel"),
        name=name,
    )(*args)


def _fourier_ctx_kernel(u_ref, fg_ref, cs_ref, cl_ref, sl_ref, wf_ref, o_ref, re_ref):
    nb, seq_len, gc = u_ref.shape
    res = _dot(u_ref[...].reshape(nb * seq_len, gc), cs_ref[...])
    for i in range(nb):
        rs = slice(i * seq_len, (i + 1) * seq_len)
        uc = res[rs, :gc].astype(BF16)
        us = res[rs, gc:].astype(BF16)
        re_ref[rs, :] = (_dot(cl_ref[...], uc) - _dot(sl_ref[...], us)).astype(BF16)
    fg = fg_ref[...].reshape(nb * seq_len, gc).astype(F32)
    out = _dot(re_ref[...], wf_ref[...]) * (fg * jax.nn.sigmoid(fg))
    o_ref[...] = out.astype(BF16).reshape(nb, seq_len, gc)


def _fourier_ctx(p3, cs, cl, sl, wf, nb=4):
    bsz, seq_len, _ = p3.shape
    gc = wf.shape[-1]
    return pl.pallas_call(
        _fourier_ctx_kernel,
        grid=(bsz // nb, F_GROUPS),
        in_specs=[
            pl.BlockSpec((nb, seq_len, gc), lambda b, g: (b, 0, g)),
            pl.BlockSpec((nb, seq_len, gc), lambda b, g: (b, 0, F_GROUPS + g)),
            pl.BlockSpec((gc, 2 * gc), lambda b, g: (0, 0)),
            pl.BlockSpec((seq_len, seq_len), lambda b, g: (0, 0)),
            pl.BlockSpec((seq_len, seq_len), lambda b, g: (0, 0)),
            pl.BlockSpec((None, gc, gc), lambda b, g: (g, 0, 0)),
        ],
        out_specs=pl.BlockSpec((nb, seq_len, gc), lambda b, g: (b, 0, g)),
        out_shape=jax.ShapeDtypeStruct((bsz, seq_len, F_GROUPS * gc), BF16),
        scratch_shapes=[pltpu.VMEM((nb * seq_len, gc), BF16)],
        compiler_params=_params("parallel", "parallel"),
        name="fourier_ctx",
    )(p3, p3, cs, cl, sl, wf)


COL_TILE = 16
ROW_RADIX = 4


def _fourier_lat1_kernel(u_ref, mr_ref, mc_ref, y_ref):
    rows, cw, _ = u_ref.shape
    n_groups, _, _, _, gc = y_ref.shape
    lo = rows // ROW_RADIX
    q = lo * cw
    for g in range(n_groups):
        lanes = slice(g * gc, (g + 1) * gc)
        u = [u_ref[lo * h:lo * (h + 1), :, lanes].reshape(q, gc).astype(F32) for h in range(ROW_RADIX)]
        s02, s13, a, b = u[0] + u[2], u[1] + u[3], u[0] - u[2], u[1] - u[3]
        v0 = (s02 + s13).astype(BF16)
        v2 = (s02 - s13).astype(BF16)
        ab = jnp.concatenate([a.astype(BF16), b.astype(BF16)], axis=0)
        outs = {0: _dot(mr_ref[0], v0), 2: _dot(mr_ref[1], v2), 1: _dot(mc_ref[0], ab), 3: _dot(mc_ref[1], ab)}
        for k1, out in outs.items():
            o = out.astype(BF16)
            for part in range(2):
                blk = o[part * q:(part + 1) * q].reshape(lo, cw, gc)
                for k2 in range(lo):
                    y_ref[g, part, k1 + ROW_RADIX * k2] = blk[k2]


def _row_dft_mats(n, tile):
    lo = n // ROW_RADIX
    eye = np.eye(tile)
    r_lo = np.arange(lo)
    mats = {}
    for k1 in range(ROW_RADIX):
        r_out = k1 + ROW_RADIX * np.arange(lo)
        ang = 2.0 * np.pi * ((r_out[:, None] * r_lo[None, :]) % n) / n
        mats[k1] = (np.kron(np.cos(ang) / np.sqrt(n), eye), np.kron(np.sin(ang) / np.sqrt(n), eye))
    real = np.stack([np.concatenate([mats[k][0], -mats[k][1]], axis=0) for k in (0, 2)])
    (c1, s1), (c3, s3) = mats[1], mats[3]
    cplx = np.stack([np.block([[c1, -s1], [-s1, -c1]]), np.block([[c3, s3], [-s3, c3]])])
    return real.astype(np.float32), cplx.astype(np.float32)


def _fourier_lat1(p4, mr, mc, gc):
    bsz, rows, cols, _ = p4.shape
    q = rows // ROW_RADIX * COL_TILE
    return pl.pallas_call(
        _fourier_lat1_kernel,
        grid=(bsz, cols // COL_TILE),
        in_specs=[
            pl.BlockSpec((None, rows, COL_TILE, F_GROUPS * gc), lambda b, cb: (b, 0, cb, 0)),
            pl.BlockSpec((2, 2 * q, q), lambda b, cb: (0, 0, 0)),
            pl.BlockSpec((2, 2 * q, 2 * q), lambda b, cb: (0, 0, 0)),
        ],
        out_specs=pl.BlockSpec((None, F_GROUPS, 2, rows, COL_TILE, gc), lambda b, cb: (b, 0, 0, 0, cb, 0)),
        out_shape=jax.ShapeDtypeStruct((bsz, F_GROUPS, 2, rows, cols, gc), BF16),
        compiler_params=_params("parallel", "parallel"),
        name="fourier_lat1",
    )(p4, mr, mc)


ROW_TILE = 32


def _fourier_lat2_kernel(y_ref, fg_ref, m2_ref, cs_ref, wf_ref, o_ref, zz_ref):
    _, rt, cols, gc = y_ref.shape
    sub = 2 * cols
    for s in range(rt // 2):
        rs = slice(s * sub, (s + 1) * sub)
        st = jnp.concatenate([y_ref[0, 2 * s:2 * s + 2].reshape(sub, gc),
                              y_ref[1, 2 * s:2 * s + 2].reshape(sub, gc)], axis=0)
        out = _dot(m2_ref[...], st)
        zz_ref[rs, :gc] = out[:sub].astype(BF16)
        zz_ref[rs, gc:] = out[sub:].astype(BF16)
    re = _dot(zz_ref[...], cs_ref[...])
    fg = fg_ref[...].astype(F32)
    o_ref[...] = (_dot(re.astype(BF16), wf_ref[...]) * (fg * jax.nn.sigmoid(fg))).astype(BF16)


def _fourier_lat2(y6, p2, m2, cs_stack, wf):
    bsz, _, _, rows, cols, gc = y6.shape
    tokens = ROW_TILE * cols
    tiles = rows // ROW_TILE
    return pl.pallas_call(
        _fourier_lat2_kernel,
        grid=(bsz, tiles, F_GROUPS),
        in_specs=[
            pl.BlockSpec((None, None, 2, ROW_TILE, cols, gc), lambda b, rb, g: (b, g, 0, rb, 0, 0)),
            pl.BlockSpec((tokens, gc), lambda b, rb, g: (b * tiles + rb, F_GROUPS + g)),
            pl.BlockSpec((4 * cols, 4 * cols), lambda b, rb, g: (0, 0)),
            pl.BlockSpec((2 * gc, gc), lambda b, rb, g: (0, 0)),
            pl.BlockSpec((None, gc, gc), lambda b, rb, g: (g, 0, 0)),
        ],
        out_specs=pl.BlockSpec((tokens, gc), lambda b, rb, g: (b * tiles + rb, g)),
        out_shape=jax.ShapeDtypeStruct((bsz * rows * cols, F_GROUPS * gc), BF16),
        scratch_shapes=[pltpu.VMEM((tokens, 2 * gc), BF16)],
        compiler_params=_params("parallel", "parallel", "parallel"),
        name="fourier_lat2",
    )(y6, p2, m2, cs_stack, wf)


OUT_COL_CHUNK = 1024


def _outproj_kernel(f_ref, g_ref, x_ref, mod_ref, fnw_ref, w_ref, o_ref):
    tm, fw = f_ref.shape
    d = w_ref.shape[-1]
    f = f_ref[...]
    g = g_ref[...]
    ss = jnp.zeros((tm, 1), F32)
    for c0 in range(0, d, OUT_COL_CHUNK):
        cs = slice(c0, c0 + OUT_COL_CHUNK)
        acc = _dot(f, w_ref[:fw, cs]) + _dot(g, w_ref[fw:, cs])
        pre = x_ref[:, cs] + mod_ref[2:3, cs] * acc
        ss = ss + jnp.sum(pre * pre, axis=-1, keepdims=True)
        o_ref[:, cs] = pre
    scale = lax.rsqrt(ss * (1.0 / d) + EPS)
    for c0 in range(0, d, OUT_COL_CHUNK):
        cs = slice(c0, c0 + OUT_COL_CHUNK)
        o_ref[:, cs] = (o_ref[:, cs] * scale) * fnw_ref[:, cs]


def _outproj(f2, g2, x2, mod3, fnw, w_bf, mod_row, name, tm=256):
    t, d = x2.shape
    fw = f2.shape[-1]
    return pl.pallas_call(
        _outproj_kernel,
        grid=(t // tm,),
        in_specs=[
            pl.BlockSpec((tm, fw), lambda i: (i, 0)),
            pl.BlockSpec((tm, d - fw), lambda i: (i, 0)),
            pl.BlockSpec((tm, d), lambda i: (i, 0)),
            pl.BlockSpec((None, 3, d), lambda i: (mod_row(i), 0, 0)),
            pl.BlockSpec((1, d), lambda i: (0, 0)),
            pl.BlockSpec((d, d), lambda i: (0, 0), pipeline_mode=pl.Buffered(1)),
        ],
        out_specs=pl.BlockSpec((tm, d), lambda i: (i, 0)),
        out_shape=jax.ShapeDtypeStruct((t, d), F32),
        compiler_params=_params("parallel"),
        name=name,
    )(f2, g2, x2, mod3, fnw, w_bf)


def kernel(x_prompt, x_sample, state_gla_fwd, state_gla_bwd, c, c_ctx, ada_w, ada_b, norm_w,
           w_in, w_alpha, b_alpha, w_fourier, gla_norm_w, w_out, final_norm_w):
    bp, lp, d = x_prompt.shape
    bs, ls, _ = x_sample.shape
    depth = ada_w.shape[0]
    assert depth == 1, "single-layer step"
    fw = d // 2
    gc = fw // F_GROUPS
    dk_total = w_alpha.shape[-1]
    n_main = 2 * fw + 2 * dk_total + 2 * (d - fw)
    gla_offsets = (2 * fw, 2 * fw + dk_total, 2 * fw + 2 * dk_total, 2 * fw + 2 * dk_total + (d - fw))
    rows = ls // GRID_W
    assert bs + 1 <= MOD_ROWS and rows == GRID_W and 8 * GLA_LOWRANK == LANES

    cc = jnp.zeros((MOD_ROWS, d), F32).at[:bs].set(c).at[bs].set(c_ctx)
    mod3 = _modulation(cc, ada_w[0], ada_b).reshape(MOD_ROWS, 3, d)

    w_all = w_in[0].astype(BF16)
    w_a = jnp.tile(w_in[0, :, n_main:], (1, LANES // (2 * GLA_LOWRANK))).astype(BF16)
    w_out_bf = w_out[0].astype(BF16)
    wf_bf = w_fourier[0].astype(BF16)
    nw = norm_w[0][None, :]
    fnw = final_norm_w[None, :]
    lr = GLA_LOWRANK
    wdir = jnp.zeros((2, 2 * lr, dk_total), F32).at[0, :lr].set(w_alpha[0, 0]).at[1, lr:].set(w_alpha[0, 1])
    wdir_hi = wdir.astype(BF16)
    wdir_lo = (wdir - wdir_hi.astype(F32)).astype(BF16)
    wz = jnp.concatenate([wdir_hi, wdir_hi, wdir_lo, jnp.zeros_like(wdir_hi)], axis=1)
    bal = b_alpha[0][:, None, :]
    gnw = gla_norm_w[0][:, None, :]
    pos_in_block = np.arange(GLA_BLOCK)
    same_chunk = (pos_in_block[:, None] // GLA_CHUNK) == (pos_in_block[None, :] // GLA_CHUNK)
    lower = pos_in_block[:, None] >= pos_in_block[None, :]
    tri = jnp.asarray(np.stack([same_chunk & lower, same_chunk & lower.T]).astype(np.float32)).astype(BF16)

    cc_c, cc_s = _dft_mats(gc)
    cs = jnp.asarray(np.concatenate([cc_c, cc_s], axis=1)).astype(BF16)
    cs_stack = jnp.asarray(np.concatenate([cc_c, cc_s], axis=0)).astype(BF16)
    cl_c, cl_s = _dft_mats(lp)
    cl = jnp.asarray(cl_c).astype(BF16)
    sl = jnp.asarray(cl_s).astype(BF16)
    g_c, g_s = _dft_mats(GRID_W)
    eye2 = np.eye(2, dtype=np.float32)
    a2, b2 = np.kron(eye2, g_c), np.kron(eye2, g_s)
    m2 = jnp.asarray(np.block([[a2, b2], [-b2, a2]])).astype(BF16)
    row_real, row_cplx = _row_dft_mats(GRID_W, COL_TILE)
    mr, mc = jnp.asarray(row_real).astype(BF16), jnp.asarray(row_cplx).astype(BF16)

    xp2 = x_prompt.reshape(bp * lp, d)
    ctx_row = lambda i: bs
    p_ctx, a_ctx = _inproj(xp2, mod3, nw, w_all, w_a, n_main, ctx_row, "inproj_ctx")
    p3 = p_ctx.reshape(bp, lp, n_main)
    g_ctx, new_f, new_b = _gla(p3, a_ctx.reshape(bp, lp, LANES), wz, bal, gnw, tri,
                               None, None, True, gla_offsets, GLA_HEADS, "gla_ctx")
    f_ctx = _fourier_ctx(p3, cs, cl, sl, wf_bf)
    y_prompt = _outproj(f_ctx.reshape(bp * lp, fw), g_ctx.reshape(bp * lp, d - fw),
                        xp2, mod3, fnw, w_out_bf, ctx_row, "outproj_ctx")

    xs2 = x_sample.reshape(bs * ls, d)
    tm_i = 1024
    p_lat, a_lat = _inproj(xs2, mod3, nw, w_all, w_a, n_main,
                           lambda i: i // (ls // tm_i), "inproj_lat", tm=tm_i)
    (g_lat,) = _gla(p_lat.reshape(bs, ls, n_main), a_lat.reshape(bs, ls, LANES), wz, bal, gnw, tri,
                    state_gla_fwd, state_gla_bwd, False, gla_offsets, 1, "gla_lat")
    y6 = _fourier_lat1(p_lat.reshape(bs, rows, GRID_W, n_main), mr, mc, gc)
    f_lat = _fourier_lat2(y6, p_lat, m2, cs_stack, wf_bf)
    tm_o = 256
    y_sample = _outproj(f_lat, g_lat.reshape(bs * ls, d - fw), xs2, mod3, fnw,
                        w_out_bf, lambda i: i // (ls // tm_o), "outproj_lat", tm=tm_o)

    return (y_prompt.reshape(bp, lp, d), y_sample.reshape(bs, ls, d), new_f, new_b)
```
